```python
import jax
import jax.numpy as jnp
from jax import lax
import numpy as np

D_MODEL = 1024
BATCH = 4
SEQ = 8192
DEPTH = 4

CHUNK = 64
N_MIXERS = 4
Q_BLOCK = 128
NORM_EPS = 1e-6
D_FF = 4 * D_MODEL

SB_HEADS = 16
SB_HEAD_DIM = D_MODEL // SB_HEADS
HG_HEADS = 8
HG_KEY_DIM = D_MODEL // HG_HEADS
HG_VAL_DIM = D_MODEL // HG_HEADS
RW_HEAD_DIM = 64
RW_HEADS = D_MODEL // RW_HEAD_DIM
RW_DECAY_LORA = 64
RW_AAA_LORA = 64
RW_GATE_LORA = 128
RW_GN_EPS = 64e-5
RW_DECAY_OFFSET = 0.5
SC_WIDTH = 3

N_SB = (DEPTH + N_MIXERS - 1) // N_MIXERS
N_HG = (DEPTH + N_MIXERS - 2) // N_MIXERS
N_RW = (DEPTH + N_MIXERS - 3) // N_MIXERS
N_SC = (DEPTH + N_MIXERS - 4) // N_MIXERS

kernel_name = 'hybrid_stickbreak_hgrn2_rwkv7_shortconv_trunk'


def rms_norm(x, g):
    xf = x.astype(jnp.float32)
    y = xf * lax.rsqrt(jnp.mean(jnp.square(xf), axis=-1, keepdims=True) + NORM_EPS)
    return (y * g.astype(jnp.float32)).astype(x.dtype)


def squared_relu_mlp(x, w_up, w_down):
    return jnp.square(jax.nn.relu(x @ w_up)) @ w_down


def stick_breaking_mixer(xn, w_qkv, w_out):
    B, S, _ = xn.shape
    qkv = xn @ w_qkv
    q, k, v = jnp.split(qkv, 3, axis=-1)
    heads = lambda t: t.reshape(B, S, SB_HEADS, SB_HEAD_DIM).transpose(0, 2, 1, 3)
    q, k, v = heads(q), heads(k), heads(v)
    scale = SB_HEAD_DIM ** -0.5
    outs = []
    for blk in range(S // Q_BLOCK):
        start = blk * Q_BLOCK
        end = start + Q_BLOCK
        kb, vb = k[:, :, :end], v[:, :, :end]
        z = jnp.einsum('bhqd,bhkd->bhqk', q[:, :, start:end], kb).astype(jnp.float32) * scale
        mask = jnp.arange(end)[None, :] < (start + jnp.arange(Q_BLOCK))[:, None]
        log_rest = jnp.where(mask, jax.nn.log_sigmoid(-z), 0.0)
        suffix = lax.cumsum(log_rest, axis=3, reverse=True) - log_rest
        attn = jnp.where(mask, jnp.exp(jax.nn.log_sigmoid(z) + suffix), 0.0)
        outs.append(jnp.einsum('bhqk,bhkd->bhqd', attn.astype(vb.dtype), vb))
    o = jnp.concatenate(outs, axis=2).transpose(0, 2, 1, 3).reshape(B, S, D_MODEL)
    return o @ w_out


def hgrn_lower_bound(table, layer_idx):
    p = jax.nn.softmax(table.astype(jnp.float32), axis=0)
    c = jnp.cumsum(p, axis=0)
    return c[layer_idx] - c[0]


def hgrn2_mixer(xn, w_in, lb, norm_g, w_out):
    B, S, _ = xn.shape
    nc = S // CHUNK
    proj = (xn @ w_in).astype(jnp.float32)
    q, fz, iv, g = jnp.split(proj, 4, axis=-1)
    q = jax.nn.silu(q)
    log_f = jnp.logaddexp(jnp.log(lb), jnp.log1p(-lb) + jax.nn.log_sigmoid(fz))
    kf = -jnp.expm1(log_f)

    def chunks(t, dim):
        return t.reshape(B, nc, CHUNK, HG_HEADS, dim).transpose(1, 0, 3, 2, 4)

    causal = jnp.tril(jnp.ones((CHUNK, CHUNK), dtype=bool))

    def step(state, inp):
        qc, lfc, kc, vc = inp
        b = lax.cumsum(lfc, axis=2)
        inter = jnp.einsum('bhtk,bhkv->bhtv', qc * jnp.exp(b), state)
        diff = b[:, :, :, None, :] - b[:, :, None, :, :]
        decay = jnp.exp(jnp.where(causal[None, None, :, :, None], diff, -jnp.inf))
        scores = jnp.einsum('bhtk,bhtsk,bhsk->bhts', qc, decay, kc)
        intra = jnp.einsum('bhts,bhsv->bhtv', scores, vc)
        b_last = b[:, :, -1:, :]
        new_state = (jnp.exp(b_last[:, :, 0, :])[..., None] * state
                     + jnp.einsum('bhsk,bhsv->bhkv', kc * jnp.exp(b_last - b), vc))
        return new_state, inter + intra

    s0 = jnp.zeros((B, HG_HEADS, HG_KEY_DIM, HG_VAL_DIM), jnp.float32)
    _, o = lax.scan(step, s0, (chunks(q, HG_KEY_DIM), chunks(log_f, HG_KEY_DIM),
                               chunks(kf, HG_KEY_DIM), chunks(iv, HG_VAL_DIM)))
    o = o.transpose(1, 0, 3, 2, 4).reshape(B, S, HG_HEADS, HG_VAL_DIM)
    o = o * lax.rsqrt(jnp.mean(jnp.square(o), axis=-1, keepdims=True) + NORM_EPS)
    o = o.reshape(B, S, D_MODEL) * norm_g * jax.nn.silu(g)
    return (o @ w_out).astype(xn.dtype)


def rwkv7_mixer(xn, mix, w_in, w0, w1, w2, a0, a1, a2, g1, g2, k_k, k_a, r_k, ln_g, ln_b, w_out):
    B, S, _ = xn.shape
    xf = xn.astype(jnp.float32)
    xx = jnp.pad(xf, ((0, 0), (1, 0), (0, 0)))[:, :S] - xf
    lerp = lambda c: xf + xx * mix[c]
    r = lerp(0) @ w_in[0]
    k = lerp(1) @ w_in[1]
    v = lerp(2) @ w_in[2]
    w_log = -jax.nn.softplus(-(w0 + jnp.tanh(lerp(3) @ w1) @ w2)) - RW_DECAY_OFFSET
    decay = jnp.exp(-jnp.exp(w_log))
    a = jax.nn.sigmoid(a0 + (lerp(4) @ a1) @ a2)
    g = jax.nn.sigmoid(lerp(5) @ g1) @ g2
    heads = lambda t: t.reshape(B, S, RW_HEADS, RW_HEAD_DIM)
    kk = heads(k * k_k)
    kk = kk / jnp.maximum(jnp.linalg.norm(kk, axis=-1, keepdims=True), 1e-12)
    k = k * (1.0 + (a - 1.0) * k_a)
    r_h, k_h, v_h, w_h, a_h = heads(r), heads(k), heads(v), heads(decay), heads(a)
    tm = lambda t: jnp.swapaxes(t, 0, 1)

    def step(state, inp):
        r_t, w_t, k_t, v_t, kk_t, a_t = inp
        sa = jnp.einsum('bhvk,bhk->bhv', state, -kk_t)
        state = (state * w_t[:, :, None, :]
                 + sa[..., None] * (kk_t * a_t)[:, :, None, :]
                 + v_t[..., None] * k_t[:, :, None, :])
        return state, jnp.einsum('bhvk,bhk->bhv', state, r_t)

    s0 = jnp.zeros((B, RW_HEADS, RW_HEAD_DIM, RW_HEAD_DIM), jnp.float32)
    _, y = lax.scan(step, s0, (tm(r_h), tm(w_h), tm(k_h), tm(v_h), tm(kk), tm(a_h)))
    y = jnp.swapaxes(y, 0, 1)
    mu = jnp.mean(y, axis=-1, keepdims=True)
    var = jnp.mean(jnp.square(y - mu), axis=-1, keepdims=True)
    yn = ((y - mu) * lax.rsqrt(var + RW_GN_EPS)).reshape(B, S, D_MODEL) * ln_g + ln_b
    bonus = jnp.sum(r_h * k_h * r_k, axis=-1, keepdims=True) * v_h
    out = (yn + bonus.reshape(B, S, D_MODEL)) * g
    return (out @ w_out).astype(xn.dtype)


def short_conv_mixer(xn, w_in, conv_w, conv_b, w_out):
    bg, cg, hx = jnp.split(xn @ w_in, 3, axis=-1)
    u = cg * hx
    y = lax.conv_general_dilated(u, conv_w[:, None, :].astype(u.dtype), window_strides=(1,),
                                 padding=[(SC_WIDTH - 1, 0)],
                                 dimension_numbers=('NWC', 'WIO', 'NWC'),
                                 feature_group_count=D_MODEL) + conv_b
    return (bg * y) @ w_out


def setup_inputs(seed: int = 0) -> dict:
    key = jax.random.key(seed)
    ks = list(jax.random.split(key, 40))
    nrm = lambda shape, scale: scale * jax.random.normal(ks.pop(), shape, jnp.float32)
    D = D_MODEL
    inv = D ** -0.5
    return {
        'x': nrm((BATCH, SEQ, D), 1.0),
        'norm_mix_g': 1.0 + nrm((DEPTH, D), 0.02),
        'norm_ffn_g': 1.0 + nrm((DEPTH, D), 0.02),
        'ffn_w_up': nrm((DEPTH, D, D_FF), inv),
        'ffn_w_down': nrm((DEPTH, D_FF, D), D_FF ** -0.5),
        'final_norm_g': 1.0 + nrm((D,), 0.02),
        'sb_w_qkv': nrm((N_SB, D, 3 * D), inv),
        'sb_w_out': nrm((N_SB, D, D), inv),
        'hg_w_in': nrm((N_HG, D, 4 * D), inv),
        'hg_lower_bounds': nrm((DEPTH, D), 0.1),
        'hg_norm_g': 1.0 + nrm((N_HG, D), 0.02),
        'hg_w_out': nrm((N_HG, D, D), inv),
        'rw_mix': jax.random.uniform(ks.pop(), (N_RW, 6, D), jnp.float32),
        'rw_w_in': nrm((N_RW, 3, D, D), inv),
        'rw_w0': nrm((N_RW, D), 0.5),
        'rw_w1': nrm((N_RW, D, RW_DECAY_LORA), inv),
        'rw_w2': nrm((N_RW, RW_DECAY_LORA, D), 0.5 * RW_DECAY_LORA ** -0.5),
        'rw_a0': nrm((N_RW, D), 0.1),
        'rw_a1': nrm((N_RW, D, RW_AAA_LORA), inv),
        'rw_a2': nrm((N_RW, RW_AAA_LORA, D), 0.5 * RW_AAA_LORA ** -0.5),
        'rw_g1': nrm((N_RW, D, RW_GATE_LORA), inv),
        'rw_g2': nrm((N_RW, RW_GATE_LORA, D), RW_GATE_LORA ** -0.5),
        'rw_k_k': 0.85 + nrm((N_RW, D), 0.02),
        'rw_k_a': 1.0 + nrm((N_RW, D), 0.02),
        'rw_r_k': nrm((N_RW, RW_HEADS, RW_HEAD_DIM), 0.1),
        'rw_ln_g': 1.0 + nrm((N_RW, D), 0.02),
        'rw_ln_b': nrm((N_RW, D), 0.02),
        'rw_w_out': nrm((N_RW, D, D), inv),
        'sc_w_in': nrm((N_SC, D, 3 * D), inv),
        'sc_conv_w': nrm((N_SC, SC_WIDTH, D), SC_WIDTH ** -0.5),
        'sc_conv_b': nrm((N_SC, D), 0.02),
        'sc_w_out': nrm((N_SC, D, D), inv),
    }


def reference(x, norm_mix_g, norm_ffn_g, ffn_w_up, ffn_w_down, final_norm_g,
              sb_w_qkv, sb_w_out, hg_w_in, hg_lower_bounds, hg_norm_g, hg_w_out,
              rw_mix, rw_w_in, rw_w0, rw_w1, rw_w2, rw_a0, rw_a1, rw_a2, rw_g1, rw_g2,
              rw_k_k, rw_k_a, rw_r_k, rw_ln_g, rw_ln_b, rw_w_out,
              sc_w_in, sc_conv_w, sc_conv_b, sc_w_out):
    h = x
    for i in range(DEPTH):
        m, j = i % N_MIXERS, i // N_MIXERS
        xn = rms_norm(h, norm_mix_g[i])
        if m == 0:
            y = stick_breaking_mixer(xn, sb_w_qkv[j], sb_w_out[j])
        elif m == 1:
            lb = hgrn_lower_bound(hg_lower_bounds, i)
            y = hgrn2_mixer(xn, hg_w_in[j], lb, hg_norm_g[j], hg_w_out[j])
        elif m == 2:
            y = rwkv7_mixer(xn, rw_mix[j], rw_w_in[j], rw_w0[j], rw_w1[j], rw_w2[j],
                            rw_a0[j], rw_a1[j], rw_a2[j], rw_g1[j], rw_g2[j],
                            rw_k_k[j], rw_k_a[j], rw_r_k[j], rw_ln_g[j], rw_ln_b[j], rw_w_out[j])
        else:
            y = short_conv_mixer(xn, sc_w_in[j], sc_conv_w[j], sc_conv_b[j], sc_w_out[j])
        h = h + y
        h = h + squared_relu_mlp(rms_norm(h, norm_ffn_g[i]), ffn_w_up[i], ffn_w_down[i])
    return rms_norm(h, final_norm_g)
```

```python
import functools

import jax
import jax.numpy as jnp
from jax import lax
from jax.experimental import pallas as pl
from jax.experimental.pallas import tpu as pltpu

F32 = jnp.float32
BF16 = jnp.bfloat16

NORM_EPS = 1e-6
RW_GN_EPS = 64e-5
RW_DECAY_OFFSET = 0.5
SB_HEAD_DIM = 64
RW_HEAD_DIM = 64
HG_HEAD_DIM = 128

V7X_LANES = 128
V7X_VMEM_LIMIT_BYTES = 56 * 1024 * 1024

EXP_ZERO_BELOW = -105.0


def _params(*sem):
    return pltpu.CompilerParams(dimension_semantics=sem, vmem_limit_bytes=V7X_VMEM_LIMIT_BYTES)


def _dot(a, b):
    return jnp.dot(a, b, preferred_element_type=F32)


def _dot_nt(a, b):
    return lax.dot_general(a, b, (((1,), (1,)), ((), ())), preferred_element_type=F32)


def _dot_tn(a, b):
    return lax.dot_general(a, b, (((0,), (0,)), ((), ())), preferred_element_type=F32)


def _split2(x):
    hi = x.astype(BF16)
    lo = (x - hi.astype(F32)).astype(BF16)
    return hi, lo


def _split3(x):
    x1 = x.astype(BF16)
    r1 = x - x1.astype(F32)
    x2 = r1.astype(BF16)
    x3 = (r1 - x2.astype(F32)).astype(BF16)
    return x1, x2, x3


def _dot_exact_lhs(m_bf16, x, dot=_dot):
    x1, x2, x3 = _split3(x)
    return dot(m_bf16, x1) + dot(m_bf16, x2) + dot(m_bf16, x3)


def _dot_exact_rhs(x, m_bf16, dot=_dot):
    hi, lo = _split2(x)
    return dot(hi, m_bf16) + dot(lo, m_bf16)


def _dot_hi(a, b, dot=_dot):
    a1, a2 = _split2(a)
    b1, b2 = _split2(b)
    return dot(a1, b1) + dot(a1, b2) + dot(a2, b1)


def _rms(x, g):
    return x * lax.rsqrt(jnp.mean(jnp.square(x), axis=-1, keepdims=True) + NORM_EPS) * g


def _sigmoid(x):
    return 1.0 / (1.0 + jnp.exp(-x))


def _silu(x):
    return x * _sigmoid(x)


def _softplus(x):
    return jnp.maximum(x, 0.0) + jnp.log(1.0 + jnp.exp(-jnp.abs(x)))


def _iota(shape, dim):
    return lax.broadcasted_iota(jnp.int32, shape, dim)


def _norm_proj_kernel(h_ref, g_ref, w_ref, o_ref, xn_sc):
    @pl.when(pl.program_id(1) == 0)
    def _():
        xn_sc[...] = _rms(h_ref[...], g_ref[...]).astype(BF16)

    o_ref[...] = _dot(xn_sc[...], w_ref[...])


def _norm_proj(h, g, w, *, tm, tn):
    t, d = h.shape
    n = w.shape[1]
    tm, tn = min(tm, t), min(tn, n)
    return pl.pallas_call(
        _norm_proj_kernel,
        grid=(t // tm, n // tn),
        in_specs=[
            pl.BlockSpec((tm, d), lambda i, j: (i, 0)),
            pl.BlockSpec((1, d), lambda i, j: (0, 0)),
            pl.BlockSpec((d, tn), lambda i, j: (0, j)),
        ],
        out_specs=pl.BlockSpec((tm, tn), lambda i, j: (i, j)),
        out_shape=jax.ShapeDtypeStruct((t, n), F32),
        scratch_shapes=[pltpu.VMEM((tm, d), BF16)],
        name="norm_proj",
        compiler_params=_params("parallel", "arbitrary"),
    )(h, g, w)


def _mlp_kernel(res_ref, y_ref, wo_ref, g_ref, wup_ref, wdn_ref, fg_ref, o_ref, xn_sc, acc_sc, *, final_norm):
    f = pl.program_id(1)

    @pl.when(f == 0)
    def _():
        h = res_ref[...] + _dot(y_ref[...].astype(BF16), wo_ref[...])
        acc_sc[...] = h
        xn_sc[...] = _rms(h, g_ref[...]).astype(BF16)

    u = _dot(xn_sc[...], wup_ref[...])
    a = jnp.square(jnp.maximum(u, 0.0)).astype(BF16)
    acc_sc[...] += _dot(a, wdn_ref[...])

    @pl.when(f == pl.num_programs(1) - 1)
    def _():
        out = acc_sc[...]
        if final_norm:
            out = _rms(out, fg_ref[...])
        o_ref[...] = out


def _mixer_out_mlp(res, y, w_o, g_ffn, w_up, w_down, final_g, *, final_norm, tm, tf):
    t, d = res.shape
    dff = w_up.shape[1]
    tm, tf = min(tm, t), min(tf, dff)
    return pl.pallas_call(
        functools.partial(_mlp_kernel, final_norm=final_norm),
        grid=(t // tm, dff // tf),
        in_specs=[
            pl.BlockSpec((tm, d), lambda i, f: (i, 0)),
            pl.BlockSpec((tm, d), lambda i, f: (i, 0)),
            pl.BlockSpec((d, d), lambda i, f: (0, 0)),
            pl.BlockSpec((1, d), lambda i, f: (0, 0)),
            pl.BlockSpec((d, tf), lambda i, f: (0, f)),
            pl.BlockSpec((tf, d), lambda i, f: (f, 0)),
            pl.BlockSpec((1, d), lambda i, f: (0, 0)),
        ],
        out_specs=pl.BlockSpec((tm, d), lambda i, f: (i, 0)),
        out_shape=jax.ShapeDtypeStruct((t, d), F32),
        scratch_shapes=[pltpu.VMEM((tm, d), BF16), pltpu.VMEM((tm, d), F32)],
        name="mixer_out_mlp",
        compiler_params=_params("parallel", "arbitrary"),
    )(res, y, w_o, g_ffn, w_up, w_down, final_g)


def _sb_kernel(q_ref, k_ref, v_ref, o_ref, acc_sc, carry_sc, *, tq, scale):
    qi = pl.program_id(2)
    lane = _iota((tq, V7X_LANES), 1)
    in_head = [lane < SB_HEAD_DIM, lane >= SB_HEAD_DIM]
    q = q_ref[0] * scale
    qh = [jnp.where(m, q, 0.0).astype(BF16) for m in in_head]
    row = _iota((tq, tq), 0)
    col = _iota((tq, tq), 1)
    later = (row > col).astype(BF16)
    causal = col < row

    acc_sc[...] = jnp.zeros_like(acc_sc)
    carry_sc[...] = jnp.zeros_like(carry_sc)

    def tile(kb, diag):
        start = pl.multiple_of(kb * tq, tq)
        kblk = k_ref[0, pl.ds(start, tq), :].astype(BF16)
        vblk = v_ref[0, pl.ds(start, tq), :].astype(BF16)
        for hd in range(2):
            z = _dot_nt(qh[hd], kblk)
            log_rest = -_softplus(z)
            log_beta = z + log_rest
            if diag:
                log_rest = jnp.where(causal, log_rest, 0.0)
            carry = carry_sc[hd]
            suffix = _dot_exact_rhs(log_rest, later) + carry
            p = jnp.exp(log_beta + suffix)
            if diag:
                p = jnp.where(causal, p, 0.0)
            acc_sc[hd] += _dot(p.astype(BF16), vblk)
            carry_sc[hd] = carry + jnp.sum(log_rest, axis=-1, keepdims=True)

    def alive():
        return jnp.max(carry_sc[...]) > EXP_ZERO_BELOW

    tile(qi, True)

    def cond(c):
        kb, go = c
        return jnp.logical_and(kb >= 0, go)

    def body(c):
        kb, _ = c
        tile(kb, False)
        return kb - 1, alive()

    lax.while_loop(cond, body, (qi - 1, alive()))
    o_ref[0] = jnp.where(in_head[0], acc_sc[0], acc_sc[1])


def _sb_attention(qkv, *, n_heads, tq):
    b, s, d3 = qkv.shape
    d = d3 // 3
    n_pairs = d // V7X_LANES
    tq = min(tq, s)
    assert d // n_heads == SB_HEAD_DIM and 2 * SB_HEAD_DIM == V7X_LANES
    return pl.pallas_call(
        functools.partial(_sb_kernel, tq=tq, scale=SB_HEAD_DIM ** -0.5),
        grid=(b, n_pairs, s // tq),
        in_specs=[
            pl.BlockSpec((1, tq, V7X_LANES), lambda bi, p, qi: (bi, qi, p)),
            pl.BlockSpec((1, s, V7X_LANES), lambda bi, p, qi: (bi, 0, n_pairs + p)),
            pl.BlockSpec((1, s, V7X_LANES), lambda bi, p, qi: (bi, 0, 2 * n_pairs + p)),
        ],
        out_specs=pl.BlockSpec((1, tq, V7X_LANES), lambda bi, p, qi: (bi, qi, p)),
        out_shape=jax.ShapeDtypeStruct((b, s, d), F32),
        scratch_shapes=[pltpu.VMEM((2, tq, V7X_LANES), F32), pltpu.VMEM((2, tq, 1), F32)],
        name="sb_attention",
        compiler_params=_params("parallel", "parallel", "arbitrary"),
    )(qkv, qkv, qkv)


def _hg_proj_kernel(h_ref, g_ref, lbt_ref, w_ref, q_ref, lf_ref, kf_ref, v_ref, gate_ref, xn_sc, *, layer_idx):
    j = pl.program_id(1)

    @pl.when(j == 0)
    def _():
        xn_sc[...] = _rms(h_ref[...], g_ref[...]).astype(BF16)

    y = _dot(xn_sc[...], w_ref[...])

    @pl.when(j == 0)
    def _():
        q_ref[...] = _silu(y)

    @pl.when(j == 1)
    def _():
        tbl = lbt_ref[...]
        e = jnp.exp(tbl - jnp.max(tbl, axis=0, keepdims=True))
        lb = jnp.sum(e[1:layer_idx + 1], axis=0, keepdims=True) / jnp.sum(e, axis=0, keepdims=True)
        sig = _sigmoid(y)
        lf_ref[...] = jnp.log(lb + (1.0 - lb) * sig)
        kf_ref[...] = (1.0 - lb) * (1.0 - sig)

    @pl.when(j == 2)
    def _():
        v_ref[...] = y

    @pl.when(j == 3)
    def _():
        gate_ref[...] = _silu(y)


def _hg_proj(h, g, lb_table, w_in, *, layer_idx, tm):
    t, d = h.shape
    tm = min(tm, t)
    row = pl.BlockSpec((tm, d), lambda i, j: (i, 0))
    out = jax.ShapeDtypeStruct((t, d), F32)
    return pl.pallas_call(
        functools.partial(_hg_proj_kernel, layer_idx=layer_idx),
        grid=(t // tm, 4),
        in_specs=[
            row,
            pl.BlockSpec((1, d), lambda i, j: (0, 0)),
            pl.BlockSpec(lb_table.shape, lambda i, j: (0, 0)),
            pl.BlockSpec((d, d), lambda i, j: (0, j)),
        ],
        out_specs=[row] * 5,
        out_shape=[out] * 5,
        scratch_shapes=[pltpu.VMEM((tm, d), BF16)],
        name="hg_proj",
        compiler_params=_params("parallel", "arbitrary"),
    )(h, g, lb_table, w_in)


HG_CHUNK = 64
HG_SUB = 16


def _hg_scan_kernel(q_ref, lf_ref, kf_ref, v_ref, gate_ref, ng_ref, o_ref, st_sc, *, n_chunks):
    c_len, sub = HG_CHUNK, HG_SUB
    n_sub = c_len // sub

    @pl.when(pl.program_id(2) == 0)
    def _():
        st_sc[...] = jnp.zeros_like(st_sc)

    tri = (_iota((c_len, c_len), 0) >= _iota((c_len, c_len), 1)).astype(BF16)
    sub_row = _iota((sub, V7X_LANES), 0)

    def chunk(c, carry):
        r0 = pl.multiple_of(c * c_len, c_len)
        rows = pl.ds(r0, c_len)
        q = q_ref[0, rows, :]
        lf = lf_ref[0, rows, :]
        k = kf_ref[0, rows, :]
        v = v_ref[0, rows, :]
        b = _dot_exact_lhs(tri, lf)
        st = st_sc[...]

        blk = lambda x, i: x[i * sub:(i + 1) * sub]
        b_end = [b[(i + 1) * sub - 1:(i + 1) * sub] for i in range(n_sub)]
        b_start = [jnp.zeros_like(b_end[0])] + b_end[:-1]
        b_last = b_end[-1]
        qt = [blk(q, i) * jnp.exp(blk(b, i) - b_start[i]) for i in range(n_sub)]
        kt = [blk(k, i) * jnp.exp(b_end[i] - blk(b, i)) for i in range(n_sub)]

        outs = []
        for i in range(n_sub):
            q_i, b_i = blk(q, i), blk(b, i)
            o_i = _dot_nt((qt[i] * jnp.exp(b_start[i])).astype(BF16), st.astype(BF16))
            if i > 0:
                ks = jnp.concatenate([kt[j] * jnp.exp(b_start[i] - b_end[j]) for j in range(i)], axis=0)
                sc = _dot_nt(qt[i].astype(BF16), ks.astype(BF16))
                o_i += _dot(sc.astype(BF16), v[:i * sub].astype(BF16))
            for s in range(sub):
                r = i * sub + s
                d = jnp.minimum(b_i - b[r:r + 1], 0.0)
                w = jnp.where(sub_row >= s, jnp.exp(d), 0.0)
                score = jnp.sum(q_i * w * k[r:r + 1], axis=-1, keepdims=True)
                o_i += score * v[r:r + 1]
            outs.append(o_i)
        o = jnp.concatenate(outs, axis=0)

        kl = jnp.concatenate([kt[j] * jnp.exp(b_last - b_end[j]) for j in range(n_sub)], axis=0)
        st_sc[...] = st * jnp.exp(b_last) + _dot_tn(v.astype(BF16), kl.astype(BF16))

        on = o * lax.rsqrt(jnp.mean(jnp.square(o), axis=-1, keepdims=True) + NORM_EPS)
        o_ref[0, rows, :] = on * ng_ref[...] * gate_ref[0, rows, :]
        return carry

    lax.fori_loop(0, n_chunks, chunk, 0)


def _hg_scan(q, lf, kf, v, gate, norm_g, *, rows_per_step):
    b, s, d = q.shape
    n_heads = d // HG_HEAD_DIM
    assert HG_HEAD_DIM == V7X_LANES
    cb = min(rows_per_step, s)
    blk = pl.BlockSpec((1, cb, V7X_LANES), lambda bi, hd, c: (bi, c, hd))
    return pl.pallas_call(
        functools.partial(_hg_scan_kernel, n_chunks=cb // HG_CHUNK),
        grid=(b, n_heads, s // cb),
        in_specs=[blk] * 5 + [pl.BlockSpec((1, V7X_LANES), lambda bi, hd, c: (0, hd))],
        out_specs=blk,
        out_shape=jax.ShapeDtypeStruct((b, s, d), F32),
        scratch_shapes=[pltpu.VMEM((V7X_LANES, V7X_LANES), F32)],
        name="hg_scan",
        compiler_params=_params("parallel", "parallel", "arbitrary"),
    )(q, lf, kf, v, gate, norm_g)


def _head_sum_matrix(value):
    r = _iota((V7X_LANES, V7X_LANES), 0) // RW_HEAD_DIM
    c = _iota((V7X_LANES, V7X_LANES), 1) // RW_HEAD_DIM
    return jnp.where(r == c, value, 0.0).astype(BF16)


def _rw_proj_kernel(h_ref, hp_ref, g_ref, mix_ref, wr_ref, wk_ref, wv_ref, w0_ref, w1_ref, w2_ref,
                    a0_ref, a1_ref, a2_ref, g1_ref, g2_ref, kk_ref, ka_ref,
                    r_out, ld_out, k_out, v_out, kkn_out, a_out, g_out, *, tm, seq):
    i = pl.program_id(0)
    gn = g_ref[...]
    xn = _rms(h_ref[...], gn)
    prev = _rms(hp_ref[...], gn)[7:8]
    prev = jnp.where((i * tm) % seq == 0, 0.0, prev)
    shifted = jnp.where(_iota(xn.shape, 0) == 0, prev, pltpu.roll(xn, 1, 0))
    xx = shifted - xn
    lerp = lambda c: (xn + xx * mix_ref[c:c + 1]).astype(BF16)

    r = _dot(lerp(0), wr_ref[...])
    k = _dot(lerp(1), wk_ref[...])
    v = _dot(lerp(2), wv_ref[...])
    wl = w0_ref[...] + _dot(jnp.tanh(_dot(lerp(3), w1_ref[...])).astype(BF16), w2_ref[...])
    w_log = -_softplus(-wl) - RW_DECAY_OFFSET
    a = _sigmoid(a0_ref[...] + _dot(_dot(lerp(4), a1_ref[...]).astype(BF16), a2_ref[...]))
    g = _dot(_sigmoid(_dot(lerp(5), g1_ref[...])).astype(BF16), g2_ref[...])

    kk = k * kk_ref[...]
    ones = _head_sum_matrix(1.0)
    parts = []
    for p in range(kk.shape[1] // V7X_LANES):
        kp = kk[:, p * V7X_LANES:(p + 1) * V7X_LANES]
        nrm = jnp.sqrt(_dot_exact_rhs(kp * kp, ones))
        parts.append(kp / jnp.maximum(nrm, 1e-12))
    r_out[...] = r
    ld_out[...] = -jnp.exp(w_log)
    k_out[...] = k * (1.0 + (a - 1.0) * ka_ref[...])
    v_out[...] = v
    kkn_out[...] = jnp.concatenate(parts, axis=1)
    a_out[...] = a
    g_out[...] = g


def _rw_proj(h, g, mix, w_r, w_k, w_v, w0, w1, w2, a0, a1, a2, g1, g2, k_k, k_a, *, seq, tm):
    t, d = h.shape
    tm = min(tm, t)
    row = pl.BlockSpec((tm, d), lambda i: (i, 0))
    full = lambda x: pl.BlockSpec(x.shape, lambda i: (0,) * x.ndim)
    prev = pl.BlockSpec((8, d), lambda i: (jnp.maximum(i * (tm // 8) - 1, 0), 0))
    out = jax.ShapeDtypeStruct((t, d), F32)
    small = (g, mix, w_r, w_k, w_v, w0, w1, w2, a0, a1, a2, g1, g2, k_k, k_a)
    return pl.pallas_call(
        functools.partial(_rw_proj_kernel, tm=tm, seq=seq),
        grid=(t // tm,),
        in_specs=[row, prev] + [full(x) for x in small],
        out_specs=[row] * 7,
        out_shape=[out] * 7,
        name="rw_proj",
        compiler_params=_params("parallel"),
    )(h, h, *small)


RW_CHUNK = 64


def _rw_scan_kernel(r_ref, ld_ref, k_ref, v_ref, kk_ref, a_ref, g_ref, rk_ref, lng_ref, lnb_ref,
                    o_ref, st_sc):
    c_len = RW_CHUNK
    n_pairs = r_ref.shape[2] // V7X_LANES

    @pl.when(pl.program_id(1) == 0)
    def _():
        st_sc[...] = jnp.zeros_like(st_sc)

    n2 = 2 * c_len
    ri, ci = _iota((n2, n2), 0), _iota((n2, n2), 1)
    strict = ci < ri
    incl = ci <= ri
    eye = (ci == ri).astype(F32)
    tri = (_iota((c_len, c_len), 0) >= _iota((c_len, c_len), 1)).astype(BF16)
    lane = _iota((c_len, V7X_LANES), 1)
    in_head = [lane < RW_HEAD_DIM, lane >= RW_HEAD_DIM]
    head_mean = _head_sum_matrix(1.0 / RW_HEAD_DIM)
    head_sum = _head_sum_matrix(1.0)

    def stack(x):
        return jnp.concatenate([jnp.where(m, x, 0.0) for m in in_head], axis=0)

    ld_all = ld_ref[0]
    b_all = _dot_exact_lhs(tri, ld_all)

    for p in range(n_pairs):
        cols = slice(p * V7X_LANES, (p + 1) * V7X_LANES)
        r, k, v = r_ref[0, :, cols], k_ref[0, :, cols], v_ref[0, :, cols]
        kk, a = kk_ref[0, :, cols], a_ref[0, :, cols]
        b = b_all[:, cols]
        b_prev = b - ld_all[:, cols]
        b_last = b[c_len - 1:c_len]
        beta = kk * a
        l_a = stack(-kk * jnp.exp(b_prev))
        l_r = stack(r * jnp.exp(b))
        r_b = stack(beta * jnp.exp(-b))
        r_k = stack(k * jnp.exp(-b))
        v_st = stack(v)
        st_t = st_sc[p]

        a_ab = jnp.where(strict, _dot_hi(l_a, r_b, _dot_nt), 0.0)
        a_ak = jnp.where(strict, _dot_hi(l_a, r_k, _dot_nt), 0.0)
        a_rb = jnp.where(incl, _dot_hi(l_r, r_b, _dot_nt), 0.0)
        a_rk = jnp.where(incl, _dot_hi(l_r, r_k, _dot_nt), 0.0)

        x = a_ab
        t_inv = eye + x
        for _ in range(c_len.bit_length() - 2):
            x = _dot_hi(x, x)
            t_inv = t_inv + _dot_hi(t_inv, x)

        sa = _dot_hi(t_inv, _dot_hi(l_a, st_t, _dot_nt) + _dot_hi(a_ak, v_st))
        y_st = _dot_hi(l_r, st_t, _dot_nt) + _dot_hi(a_rb, sa) + _dot_hi(a_rk, v_st)
        y = y_st[:c_len] + y_st[c_len:]

        c_b = stack(beta * jnp.exp(b_last - b))
        c_k = stack(k * jnp.exp(b_last - b))
        st_sc[p] = st_t * jnp.exp(b_last) + _dot_hi(sa, c_b, _dot_tn) + _dot_hi(v_st, c_k, _dot_tn)

        mu = _dot_exact_rhs(y, head_mean)
        dlt = y - mu
        var = _dot_exact_rhs(dlt * dlt, head_mean)
        yn = dlt * lax.rsqrt(var + RW_GN_EPS) * lng_ref[:, cols] + lnb_ref[:, cols]
        bonus = _dot_exact_rhs(r * k * rk_ref[:, cols], head_sum) * v
        o_ref[0, :, cols] = (yn + bonus) * g_ref[0, :, cols]


def _rw_scan(r, ld, k, v, kk, a, g, r_k, ln_g, ln_b):
    b, s, d = r.shape
    assert 2 * RW_HEAD_DIM == V7X_LANES
    blk = pl.BlockSpec((1, RW_CHUNK, d), lambda bi, c: (bi, c, 0))
    vec = pl.BlockSpec((1, d), lambda bi, c: (0, 0))
    return pl.pallas_call(
        _rw_scan_kernel,
        grid=(b, s // RW_CHUNK),
        in_specs=[blk] * 7 + [vec] * 3,
        out_specs=blk,
        out_shape=jax.ShapeDtypeStruct((b, s, d), F32),
        scratch_shapes=[pltpu.VMEM((d // V7X_LANES, V7X_LANES, V7X_LANES), F32)],
        name="rw_scan",
        compiler_params=_params("parallel", "arbitrary"),
    )(r, ld, k, v, kk, a, g, r_k, ln_g, ln_b)


SC_HALO = 16


def _sc_kernel(h_ref, hp_ref, g_ref, w_ref, cw_ref, cb_ref, o_ref, *, tm, seq):
    i = pl.program_id(0)
    d = h_ref.shape[1]
    gn = g_ref[...]
    xn = _rms(h_ref[...], gn)
    xe = jnp.concatenate([_rms(hp_ref[...], gn), xn], axis=0).astype(BF16)
    gate = _dot(xn.astype(BF16), w_ref[:, 0:d])
    ch = _dot(xe, w_ref[:, d:3 * d])
    u = ch[:, :d] * ch[:, d:]
    halo_is_padding = (i * tm) % seq == 0
    u = jnp.where(jnp.logical_and(_iota(u.shape, 0) < SC_HALO, halo_is_padding), 0.0, u)
    y = (cw_ref[2:3] * u[SC_HALO:]
         + cw_ref[1:2] * pltpu.roll(u, 1, 0)[SC_HALO:]
         + cw_ref[0:1] * pltpu.roll(u, 2, 0)[SC_HALO:]
         + cb_ref[...])
    o_ref[...] = gate * y


def _short_conv(h, g, w_in, conv_w, conv_b, *, seq, tm):
    t, d = h.shape
    tm = min(tm, t)
    row = pl.BlockSpec((tm, d), lambda i: (i, 0))
    full = lambda x: pl.BlockSpec(x.shape, lambda i: (0,) * x.ndim)
    prev = pl.BlockSpec((SC_HALO, d), lambda i: (jnp.maximum(i * (tm // SC_HALO) - 1, 0), 0))
    return pl.pallas_call(
        functools.partial(_sc_kernel, tm=tm, seq=seq),
        grid=(t // tm,),
        in_specs=[row, prev, full(g), full(w_in), full(conv_w), full(conv_b)],
        out_specs=row,
        out_shape=jax.ShapeDtypeStruct((t, d), F32),
        name="short_conv",
        compiler_params=_params("parallel"),
    )(h, h, g, w_in, conv_w, conv_b)


def kernel(x, norm_mix_g, norm_ffn_g, ffn_w_up, ffn_w_down, final_norm_g, sb_w_qkv, sb_w_out, hg_w_in, hg_lower_bounds, hg_norm_g, hg_w_out, rw_mix, rw_w_in, rw_w0, rw_w1, rw_w2, rw_a0, rw_a1, rw_a2, rw_g1, rw_g2, rw_k_k, rw_k_a, rw_r_k, rw_ln_g, rw_ln_b, rw_w_out, sc_w_in, sc_conv_w, sc_conv_b, sc_w_out):
    bsz, seq, d = x.shape
    depth = norm_mix_g.shape[0]
    n_mixers = 4
    t = bsz * seq
    bf = lambda w: w.astype(BF16)
    row = lambda v: v.reshape(1, -1)
    seq3 = lambda y: y.reshape(bsz, seq, d)

    h = x.reshape(t, d)
    for i in range(depth):
        m, j = i % n_mixers, i // n_mixers
        g_mix = row(norm_mix_g[i])
        if m == 0:
            qkv = _norm_proj(h, g_mix, bf(sb_w_qkv[j]), tm=1024, tn=1024)
            y = _sb_attention(qkv.reshape(bsz, seq, 3 * d), n_heads=d // SB_HEAD_DIM, tq=128)
            w_o = sb_w_out[j]
        elif m == 1:
            q, lf, kf, v, gate = _hg_proj(h, g_mix, hg_lower_bounds, bf(hg_w_in[j]), layer_idx=i, tm=512)
            y = _hg_scan(seq3(q), seq3(lf), seq3(kf), seq3(v), seq3(gate), row(hg_norm_g[j]),
                         rows_per_step=512)
            w_o = hg_w_out[j]
        elif m == 2:
            parts = _rw_proj(h, g_mix, rw_mix[j], bf(rw_w_in[j, 0]), bf(rw_w_in[j, 1]), bf(rw_w_in[j, 2]),
                             row(rw_w0[j]), bf(rw_w1[j]), bf(rw_w2[j]), row(rw_a0[j]), bf(rw_a1[j]),
                             bf(rw_a2[j]), bf(rw_g1[j]), bf(rw_g2[j]), row(rw_k_k[j]), row(rw_k_a[j]),
                             seq=seq, tm=256)
            y = _rw_scan(*[seq3(p) for p in parts], row(rw_r_k[j]), row(rw_ln_g[j]), row(rw_ln_b[j]))
            w_o = rw_w_out[j]
        else:
            y = _short_conv(h, g_mix, bf(sc_w_in[j]), sc_conv_w[j], row(sc_conv_b[j]), seq=seq, tm=512)
            w_o = sc_w_out[j]
        h = _mixer_out_mlp(h, y.reshape(t, d), bf(w_o), row(norm_ffn_g[i]), bf(ffn_w_up[i]),
                           bf(ffn_w_down[i]), row(final_norm_g), final_norm=(i == depth - 1),
                           tm=1024, tf=512)
    return h.reshape(bsz, seq, d)
```

```python
import functools

import jax
import jax.numpy as jnp
from jax import lax
from jax.experimental import pallas as pl
from jax.experimental.pallas import tpu as pltpu

F32 = jnp.float32
BF16 = jnp.bfloat16

NORM_EPS = 1e-6
RW_GN_EPS = 64e-5
RW_DECAY_OFFSET = 0.5
SB_HEAD_DIM = 64
RW_HEAD_DIM = 64
HG_HEAD_DIM = 128

V7X_LANES = 128
V7X_VMEM_LIMIT_BYTES = 56 * 1024 * 1024

EXP_ZERO_BELOW = -105.0


def _params(*sem):
    return pltpu.CompilerParams(dimension_semantics=sem, vmem_limit_bytes=V7X_VMEM_LIMIT_BYTES)


def _dot(a, b):
    return jnp.dot(a, b, preferred_element_type=F32)


def _dot_nt(a, b):
    return lax.dot_general(a, b, (((1,), (1,)), ((), ())), preferred_element_type=F32)


def _dot_tn(a, b):
    return lax.dot_general(a, b, (((0,), (0,)), ((), ())), preferred_element_type=F32)


def _split2(x):
    hi = x.astype(BF16)
    lo = (x - hi.astype(F32)).astype(BF16)
    return hi, lo


def _split3(x):
    x1 = x.astype(BF16)
    r1 = x - x1.astype(F32)
    x2 = r1.astype(BF16)
    x3 = (r1 - x2.astype(F32)).astype(BF16)
    return x1, x2, x3


def _dot_exact_lhs(m_bf16, x, dot=_dot):
    x1, x2, x3 = _split3(x)
    return dot(m_bf16, x1) + dot(m_bf16, x2) + dot(m_bf16, x3)


def _dot_exact_rhs(x, m_bf16, dot=_dot):
    hi, lo = _split2(x)
    return dot(hi, m_bf16) + dot(lo, m_bf16)


def _dot_hi(a, b, dot=_dot):
    a1, a2 = _split2(a)
    b1, b2 = _split2(b)
    return dot(a1, b1) + dot(a1, b2) + dot(a2, b1)


def _rms(x, g):
    return x * lax.rsqrt(jnp.mean(jnp.square(x), axis=-1, keepdims=True) + NORM_EPS) * g


def _sigmoid(x):
    return 1.0 / (1.0 + jnp.exp(-x))


def _silu(x):
    return x * _sigmoid(x)


def _softplus(x):
    return jnp.maximum(x, 0.0) + jnp.log(1.0 + jnp.exp(-jnp.abs(x)))


def _iota(shape, dim):
    return lax.broadcasted_iota(jnp.int32, shape, dim)


def _norm_proj_kernel(h_ref, g_ref, w_ref, o_ref, xn_sc):
    @pl.when(pl.program_id(1) == 0)
    def _():
        xn_sc[...] = _rms(h_ref[...], g_ref[...]).astype(BF16)

    o_ref[...] = _dot(xn_sc[...], w_ref[...]).astype(o_ref.dtype)


def _norm_proj(h, g, w, *, out_dtype, tm, tn):
    t, d = h.shape
    n = w.shape[1]
    tm, tn = min(tm, t), min(tn, n)
    return pl.pallas_call(
        _norm_proj_kernel,
        grid=(t // tm, n // tn),
        in_specs=[
            pl.BlockSpec((tm, d), lambda i, j: (i, 0)),
            pl.BlockSpec((1, d), lambda i, j: (0, 0)),
            pl.BlockSpec((d, tn), lambda i, j: (0, j)),
        ],
        out_specs=pl.BlockSpec((tm, tn), lambda i, j: (i, j)),
        out_shape=jax.ShapeDtypeStruct((t, n), out_dtype),
        scratch_shapes=[pltpu.VMEM((tm, d), BF16)],
        name="norm_proj",
        compiler_params=_params("parallel", "arbitrary"),
    )(h, g, w)


def _mlp_kernel(res_ref, y_ref, wo_ref, g_ref, wup_ref, wdn_ref, fg_ref, o_ref, xn_sc, acc_sc, *, final_norm):
    f = pl.program_id(1)

    @pl.when(f == 0)
    def _():
        h = res_ref[...] + _dot(y_ref[...].astype(BF16), wo_ref[...])
        acc_sc[...] = h
        xn_sc[...] = _rms(h, g_ref[...]).astype(BF16)

    u = _dot(xn_sc[...], wup_ref[...])
    a = jnp.square(jnp.maximum(u, 0.0)).astype(BF16)
    acc_sc[...] += _dot(a, wdn_ref[...])

    @pl.when(f == pl.num_programs(1) - 1)
    def _():
        out = acc_sc[...]
        if final_norm:
            out = _rms(out, fg_ref[...])
        o_ref[...] = out


def _mixer_out_mlp(res, y, w_o, g_ffn, w_up, w_down, final_g, *, final_norm, tm, tf):
    t, d = res.shape
    dff = w_up.shape[1]
    tm, tf = min(tm, t), min(tf, dff)
    return pl.pallas_call(
        functools.partial(_mlp_kernel, final_norm=final_norm),
        grid=(t // tm, dff // tf),
        in_specs=[
            pl.BlockSpec((tm, d), lambda i, f: (i, 0)),
            pl.BlockSpec((tm, d), lambda i, f: (i, 0)),
            pl.BlockSpec((d, d), lambda i, f: (0, 0)),
            pl.BlockSpec((1, d), lambda i, f: (0, 0)),
            pl.BlockSpec((d, tf), lambda i, f: (0, f)),
            pl.BlockSpec((tf, d), lambda i, f: (f, 0)),
            pl.BlockSpec((1, d), lambda i, f: (0, 0)),
        ],
        out_specs=pl.BlockSpec((tm, d), lambda i, f: (i, 0)),
        out_shape=jax.ShapeDtypeStruct((t, d), F32),
        scratch_shapes=[pltpu.VMEM((tm, d), BF16), pltpu.VMEM((tm, d), F32)],
        name="mixer_out_mlp",
        compiler_params=_params("parallel", "arbitrary"),
    )(res, y, w_o, g_ffn, w_up, w_down, final_g)


def _sb_kernel(q_ref, k_ref, v_ref, o_ref, acc_sc, carry_sc, *, tq, scale):
    qi = pl.program_id(2)
    lane = _iota((tq, V7X_LANES), 1)
    in_head = [lane < SB_HEAD_DIM, lane >= SB_HEAD_DIM]
    q = q_ref[0].astype(F32) * scale
    qh = [jnp.where(m, q, 0.0).astype(BF16) for m in in_head]
    row = _iota((tq, tq), 0)
    col = _iota((tq, tq), 1)
    later = (row > col).astype(BF16)
    causal = col < row

    acc_sc[...] = jnp.zeros_like(acc_sc)
    carry_sc[...] = jnp.zeros_like(carry_sc)

    def tile(kb, diag):
        start = pl.multiple_of(kb * tq, tq)
        kblk = k_ref[0, pl.ds(start, tq), :]
        vblk = v_ref[0, pl.ds(start, tq), :]
        heads = range(2)
        z = [_dot_nt(qh[hd], kblk) for hd in heads]
        log_rest = [-_softplus(z[hd]) for hd in heads]
        log_beta = [z[hd] + log_rest[hd] for hd in heads]
        if diag:
            log_rest = [jnp.where(causal, log_rest[hd], 0.0) for hd in heads]
        carry = [carry_sc[hd] for hd in heads]
        suffix = [_dot_exact_rhs(log_rest[hd], later) + carry[hd] for hd in heads]
        p = [jnp.exp(log_beta[hd] + suffix[hd]) for hd in heads]
        if diag:
            p = [jnp.where(causal, p[hd], 0.0) for hd in heads]
        for hd in heads:
            acc_sc[hd] += _dot(p[hd].astype(BF16), vblk)
            carry_sc[hd] = carry[hd] + jnp.sum(log_rest[hd], axis=-1, keepdims=True)

    def alive():
        return jnp.max(carry_sc[...]) > EXP_ZERO_BELOW

    tile(qi, True)

    def cond(c):
        kb, go = c
        return jnp.logical_and(kb >= 0, go)

    def body(c):
        kb, _ = c
        tile(kb, False)
        return kb - 1, alive()

    lax.while_loop(cond, body, (qi - 1, alive()))
    o_ref[0] = jnp.where(in_head[0], acc_sc[0], acc_sc[1])


def _sb_attention(qkv, *, n_heads, tq):
    b, s, d3 = qkv.shape
    d = d3 // 3
    n_pairs = d // V7X_LANES
    tq = min(tq, s)
    assert d // n_heads == SB_HEAD_DIM and 2 * SB_HEAD_DIM == V7X_LANES
    return pl.pallas_call(
        functools.partial(_sb_kernel, tq=tq, scale=SB_HEAD_DIM ** -0.5),
        grid=(b, n_pairs, s // tq),
        in_specs=[
            pl.BlockSpec((1, tq, V7X_LANES), lambda bi, p, qi: (bi, qi, p)),
            pl.BlockSpec((1, s, V7X_LANES), lambda bi, p, qi: (bi, 0, n_pairs + p)),
            pl.BlockSpec((1, s, V7X_LANES), lambda bi, p, qi: (bi, 0, 2 * n_pairs + p)),
        ],
        out_specs=pl.BlockSpec((1, tq, V7X_LANES), lambda bi, p, qi: (bi, qi, p)),
        out_shape=jax.ShapeDtypeStruct((b, s, d), F32),
        scratch_shapes=[pltpu.VMEM((2, tq, V7X_LANES), F32), pltpu.VMEM((2, tq, 1), F32)],
        name="sb_attention",
        compiler_params=_params("parallel", "parallel", "arbitrary"),
    )(qkv, qkv, qkv)


def _hg_proj_kernel(h_ref, g_ref, lbt_ref, w_ref, q_ref, lf_ref, kf_ref, v_ref, gate_ref, *, layer_idx):
    d = h_ref.shape[1]
    xn = _rms(h_ref[...], g_ref[...]).astype(BF16)
    part = lambda j: _dot(xn, w_ref[:, j * d:(j + 1) * d])
    q_ref[...] = _silu(part(0))
    tbl = lbt_ref[...]
    e = jnp.exp(tbl - jnp.max(tbl, axis=0, keepdims=True))
    lb = jnp.sum(e[1:layer_idx + 1], axis=0, keepdims=True) / jnp.sum(e, axis=0, keepdims=True)
    sig = _sigmoid(part(1))
    lf_ref[...] = jnp.log(lb + (1.0 - lb) * sig)
    kf_ref[...] = (1.0 - lb) * (1.0 - sig)
    v_ref[...] = part(2)
    gate_ref[...] = _silu(part(3))


def _hg_proj(h, g, lb_table, w_in, *, layer_idx, tm):
    t, d = h.shape
    tm = min(tm, t)
    row = pl.BlockSpec((tm, d), lambda i: (i, 0))
    full = lambda x: pl.BlockSpec(x.shape, lambda i: (0,) * x.ndim)
    out = jax.ShapeDtypeStruct((t, d), F32)
    return pl.pallas_call(
        functools.partial(_hg_proj_kernel, layer_idx=layer_idx),
        grid=(t // tm,),
        in_specs=[row, full(g), full(lb_table), full(w_in)],
        out_specs=[row] * 5,
        out_shape=[out] * 5,
        name="hg_proj",
        compiler_params=_params("parallel"),
    )(h, g, lb_table, w_in)


HG_CHUNK = 64


def _hg_level_matrix(c_len):
    t = _iota((c_len, c_len), 0)
    j = _iota((c_len, c_len), 1)
    le = (j <= t).astype(F32)
    blocks = [le, 1.0 - le]
    m = c_len
    while m >= 2:
        mid = (t // m) * m + m // 2 - 1
        blocks.append(le - (j <= mid).astype(F32))
        m //= 2
    return jnp.concatenate(blocks, axis=0).astype(BF16)


def _hg_scan_kernel(q_ref, lf_ref, kf_ref, v_ref, gate_ref, ng_ref, o_ref, st_sc, *, n_chunks):
    c_len = HG_CHUNK

    @pl.when(pl.program_id(2) == 0)
    def _():
        st_sc[...] = jnp.zeros_like(st_sc)

    coeff = _hg_level_matrix(c_len)
    row_l = _iota((c_len, V7X_LANES), 0)
    ti = _iota((c_len, c_len), 0)
    si = _iota((c_len, c_len), 1)
    levels = []
    m = c_len
    while m >= 2:
        upper = (row_l % m) >= m // 2
        pair = jnp.logical_and(ti // m == si // m,
                               jnp.logical_and((ti % m) >= m // 2, (si % m) < m // 2))
        levels.append((upper, pair))
        m //= 2
    on_diag = ti == si

    chunks = range(n_chunks)
    rows = [slice(c * c_len, (c + 1) * c_len) for c in chunks]
    q = [q_ref[0, rw, :] for rw in rows]
    k = [kf_ref[0, rw, :] for rw in rows]
    v = [v_ref[0, rw, :].astype(BF16) for rw in rows]
    sums = [_dot_exact_lhs(coeff, lf_ref[0, rw, :]) for rw in rows]
    b = [s[:c_len] for s in sums]
    rest = [s[c_len:2 * c_len] for s in sums]

    scores = [jnp.where(on_diag, _dot_nt(q[c].astype(BF16), k[c].astype(BF16)), 0.0) for c in chunks]
    for lv, (upper, pair) in enumerate(levels):
        for c in chunks:
            d = sums[c][(2 + lv) * c_len:(3 + lv) * c_len]
            x = (jnp.where(upper, q[c], k[c]) * jnp.exp(-jnp.abs(d))).astype(BF16)
            scores[c] = scores[c] + jnp.where(pair, _dot_nt(x, x), 0.0)
    intra = [_dot(scores[c].astype(BF16), v[c]) for c in chunks]
    kv = [_dot_tn(v[c], (k[c] * jnp.exp(rest[c])).astype(BF16)) for c in chunks]

    states = [st_sc[...]]
    for c in chunks:
        states.append(states[c] * jnp.exp(b[c][c_len - 1:c_len]) + kv[c])
    st_sc[...] = states[n_chunks]

    for c in chunks:
        o = _dot_nt((q[c] * jnp.exp(b[c])).astype(BF16), states[c].astype(BF16)) + intra[c]
        on = o * lax.rsqrt(jnp.mean(jnp.square(o), axis=-1, keepdims=True) + NORM_EPS)
        o_ref[0, rows[c], :] = on * ng_ref[...] * gate_ref[0, rows[c], :]


def _hg_scan(q, lf, kf, v, gate, norm_g, *, rows_per_step):
    b, s, d = q.shape
    n_heads = d // HG_HEAD_DIM
    assert HG_HEAD_DIM == V7X_LANES
    cb = min(rows_per_step, s)
    blk = pl.BlockSpec((1, cb, V7X_LANES), lambda bi, hd, c: (bi, c, hd))
    return pl.pallas_call(
        functools.partial(_hg_scan_kernel, n_chunks=cb // HG_CHUNK),
        grid=(b, n_heads, s // cb),
        in_specs=[blk] * 5 + [pl.BlockSpec((1, V7X_LANES), lambda bi, hd, c: (0, hd))],
        out_specs=blk,
        out_shape=jax.ShapeDtypeStruct((b, s, d), F32),
        scratch_shapes=[pltpu.VMEM((V7X_LANES, V7X_LANES), F32)],
        name="hg_scan",
        compiler_params=_params("parallel", "parallel", "arbitrary"),
    )(q, lf, kf, v, gate, norm_g)


def _head_sum_matrix(value):
    r = _iota((V7X_LANES, V7X_LANES), 0) // RW_HEAD_DIM
    c = _iota((V7X_LANES, V7X_LANES), 1) // RW_HEAD_DIM
    return jnp.where(r == c, value, 0.0).astype(BF16)


def _rw_proj_kernel(h_ref, hp_ref, g_ref, mix_ref, wr_ref, wk_ref, wv_ref, w0_ref, w1_ref, w2_ref,
                    a0_ref, a1_ref, a2_ref, g1_ref, g2_ref, kk_ref, ka_ref,
                    r_out, ld_out, k_out, v_out, kkn_out, a_out, g_out, *, tm, seq):
    i = pl.program_id(0)
    gn = g_ref[...]
    xn = _rms(h_ref[...], gn)
    prev = _rms(hp_ref[...], gn)[7:8]
    prev = jnp.where((i * tm) % seq == 0, 0.0, prev)
    shifted = jnp.where(_iota(xn.shape, 0) == 0, prev, pltpu.roll(xn, 1, 0))
    xx = shifted - xn
    lerp = lambda c: (xn + xx * mix_ref[c:c + 1]).astype(BF16)

    r = _dot(lerp(0), wr_ref[...])
    k = _dot(lerp(1), wk_ref[...])
    v = _dot(lerp(2), wv_ref[...])
    wl = w0_ref[...] + _dot(jnp.tanh(_dot(lerp(3), w1_ref[...])).astype(BF16), w2_ref[...])
    w_log = -_softplus(-wl) - RW_DECAY_OFFSET
    a = _sigmoid(a0_ref[...] + _dot(_dot(lerp(4), a1_ref[...]).astype(BF16), a2_ref[...]))
    g = _dot(_sigmoid(_dot(lerp(5), g1_ref[...])).astype(BF16), g2_ref[...])

    kk = k * kk_ref[...]
    ones = _head_sum_matrix(1.0)
    parts = []
    for p in range(kk.shape[1] // V7X_LANES):
        kp = kk[:, p * V7X_LANES:(p + 1) * V7X_LANES]
        nrm = jnp.sqrt(_dot_exact_rhs(kp * kp, ones))
        parts.append(kp / jnp.maximum(nrm, 1e-12))
    r_out[...] = r
    ld_out[...] = -jnp.exp(w_log)
    k_out[...] = k * (1.0 + (a - 1.0) * ka_ref[...])
    v_out[...] = v
    kkn_out[...] = jnp.concatenate(parts, axis=1)
    a_out[...] = a
    g_out[...] = g


def _rw_proj(h, g, mix, w_r, w_k, w_v, w0, w1, w2, a0, a1, a2, g1, g2, k_k, k_a, *, seq, tm):
    t, d = h.shape
    tm = min(tm, t)
    row = pl.BlockSpec((tm, d), lambda i: (i, 0))
    full = lambda x: pl.BlockSpec(x.shape, lambda i: (0,) * x.ndim)
    prev = pl.BlockSpec((8, d), lambda i: (jnp.maximum(i * (tm // 8) - 1, 0), 0))
    out = jax.ShapeDtypeStruct((t, d), F32)
    small = (g, mix, w_r, w_k, w_v, w0, w1, w2, a0, a1, a2, g1, g2, k_k, k_a)
    return pl.pallas_call(
        functools.partial(_rw_proj_kernel, tm=tm, seq=seq),
        grid=(t // tm,),
        in_specs=[row, prev] + [full(x) for x in small],
        out_specs=[row] * 7,
        out_shape=[out] * 7,
        name="rw_proj",
        compiler_params=_params("parallel"),
    )(h, h, *small)


RW_CHUNK = 64


def _rw_scan_kernel(r_ref, ld_ref, k_ref, v_ref, kk_ref, a_ref, g_ref, rk_ref, lng_ref, lnb_ref,
                    o_ref, st_sc):
    c_len = RW_CHUNK
    n_pairs = r_ref.shape[2] // V7X_LANES

    @pl.when(pl.program_id(1) == 0)
    def _():
        st_sc[...] = jnp.zeros_like(st_sc)

    n2 = 2 * c_len
    ri, ci = _iota((n2, n2), 0), _iota((n2, n2), 1)
    strict = ci < ri
    incl = ci <= ri
    eye = (ci == ri).astype(F32)
    tri = (_iota((c_len, c_len), 0) >= _iota((c_len, c_len), 1)).astype(BF16)
    lane = _iota((c_len, V7X_LANES), 1)
    in_head = [lane < RW_HEAD_DIM, lane >= RW_HEAD_DIM]
    head_mean = _head_sum_matrix(1.0 / RW_HEAD_DIM)
    head_sum = _head_sum_matrix(1.0)

    def stack(x):
        return jnp.concatenate([jnp.where(m, x, 0.0) for m in in_head], axis=0)

    ld_all = ld_ref[0]
    b_all = _dot_exact_lhs(tri, ld_all)

    pairs = range(n_pairs)
    cols = [slice(p * V7X_LANES, (p + 1) * V7X_LANES) for p in pairs]
    lhs, rhs, v_st, end_rows = [], [], [], []
    for p in pairs:
        r, k, kk = r_ref[0, :, cols[p]], k_ref[0, :, cols[p]], kk_ref[0, :, cols[p]]
        b = b_all[:, cols[p]]
        beta = kk * a_ref[0, :, cols[p]]
        grow, decay, to_end = jnp.exp(-b), jnp.exp(b), jnp.exp(b[c_len - 1:c_len] - b)
        lhs.append(jnp.concatenate([stack(-kk * jnp.exp(b - ld_all[:, cols[p]])), stack(r * decay)], axis=0))
        rhs.append(jnp.concatenate([stack(beta * grow), stack(k * grow)], axis=0))
        v_st.append(stack(v_ref[0, :, cols[p]]))
        end_rows.append(jnp.concatenate([stack(beta * to_end), stack(k * to_end)], axis=0))

    aa = [_dot_hi(lhs[p], rhs[p], _dot_nt) for p in pairs]
    a_ab = [jnp.where(strict, aa[p][:n2, :n2], 0.0) for p in pairs]
    a_k = [jnp.concatenate([jnp.where(strict, aa[p][:n2, n2:], 0.0),
                            jnp.where(incl, aa[p][n2:, n2:], 0.0)], axis=0).astype(BF16) for p in pairs]
    a_rb = [jnp.where(incl, aa[p][n2:, :n2], 0.0).astype(BF16) for p in pairs]

    x = a_ab
    t_inv = [eye + x[p] for p in pairs]
    for _ in range(c_len.bit_length() - 2):
        xb = [x[p].astype(BF16) for p in pairs]
        x = [_dot(xb[p], xb[p]) for p in pairs]
        t_inv = [t_inv[p] + _dot(t_inv[p].astype(BF16), x[p].astype(BF16)) for p in pairs]

    st_t = [st_sc[p] for p in pairs]
    by_state = [_dot_nt(lhs[p].astype(BF16), st_t[p].astype(BF16)) for p in pairs]
    by_v = [_dot(a_k[p], v_st[p].astype(BF16)) for p in pairs]
    sa = [_dot(t_inv[p].astype(BF16), (by_state[p][:n2] + by_v[p][:n2]).astype(BF16)) for p in pairs]
    y_st = [by_state[p][n2:] + by_v[p][n2:] + _dot(a_rb[p], sa[p].astype(BF16)) for p in pairs]
    for p in pairs:
        decay_all = jnp.exp(b_all[c_len - 1:c_len, cols[p]])
        st_sc[p] = st_t[p] * decay_all + _dot_hi(
            jnp.concatenate([sa[p], v_st[p]], axis=0), end_rows[p], _dot_tn)

    y = [y_st[p][:c_len] + y_st[p][c_len:] for p in pairs]
    sums = [_dot_exact_rhs(jnp.concatenate(
        [y[p], r_ref[0, :, cols[p]] * k_ref[0, :, cols[p]] * rk_ref[:, cols[p]]], axis=0), head_sum)
        for p in pairs]
    dlt = [y[p] - sums[p][:c_len] * (1.0 / RW_HEAD_DIM) for p in pairs]
    var = [_dot_exact_rhs(dlt[p] * dlt[p], head_mean) for p in pairs]
    for p in pairs:
        yn = dlt[p] * lax.rsqrt(var[p] + RW_GN_EPS) * lng_ref[:, cols[p]] + lnb_ref[:, cols[p]]
        o_ref[0, :, cols[p]] = (yn + sums[p][c_len:] * v_ref[0, :, cols[p]]) * g_ref[0, :, cols[p]]


def _rw_scan(r, ld, k, v, kk, a, g, r_k, ln_g, ln_b):
    b, s, d = r.shape
    assert 2 * RW_HEAD_DIM == V7X_LANES
    blk = pl.BlockSpec((1, RW_CHUNK, d), lambda bi, c: (bi, c, 0))
    vec = pl.BlockSpec((1, d), lambda bi, c: (0, 0))
    return pl.pallas_call(
        _rw_scan_kernel,
        grid=(b, s // RW_CHUNK),
        in_specs=[blk] * 7 + [vec] * 3,
        out_specs=blk,
        out_shape=jax.ShapeDtypeStruct((b, s, d), F32),
        scratch_shapes=[pltpu.VMEM((d // V7X_LANES, V7X_LANES, V7X_LANES), F32)],
        name="rw_scan",
        compiler_params=_params("parallel", "arbitrary"),
    )(r, ld, k, v, kk, a, g, r_k, ln_g, ln_b)


SC_HALO = 16


def _sc_kernel(h_ref, hp_ref, g_ref, w_ref, cw_ref, cb_ref, o_ref, *, tm, seq):
    i = pl.program_id(0)
    d = h_ref.shape[1]
    gn = g_ref[...]
    xn = _rms(h_ref[...], gn)
    xe = jnp.concatenate([_rms(hp_ref[...], gn), xn], axis=0).astype(BF16)
    gate = _dot(xn.astype(BF16), w_ref[:, 0:d])
    ch = _dot(xe, w_ref[:, d:3 * d])
    u = ch[:, :d] * ch[:, d:]
    halo_is_padding = (i * tm) % seq == 0
    u = jnp.where(jnp.logical_and(_iota(u.shape, 0) < SC_HALO, halo_is_padding), 0.0, u)
    y = (cw_ref[2:3] * u[SC_HALO:]
         + cw_ref[1:2] * pltpu.roll(u, 1, 0)[SC_HALO:]
         + cw_ref[0:1] * pltpu.roll(u, 2, 0)[SC_HALO:]
         + cb_ref[...])
    o_ref[...] = gate * y


def _short_conv(h, g, w_in, conv_w, conv_b, *, seq, tm):
    t, d = h.shape
    tm = min(tm, t)
    row = pl.BlockSpec((tm, d), lambda i: (i, 0))
    full = lambda x: pl.BlockSpec(x.shape, lambda i: (0,) * x.ndim)
    prev = pl.BlockSpec((SC_HALO, d), lambda i: (jnp.maximum(i * (tm // SC_HALO) - 1, 0), 0))
    return pl.pallas_call(
        functools.partial(_sc_kernel, tm=tm, seq=seq),
        grid=(t // tm,),
        in_specs=[row, prev, full(g), full(w_in), full(conv_w), full(conv_b)],
        out_specs=row,
        out_shape=jax.ShapeDtypeStruct((t, d), F32),
        name="short_conv",
        compiler_params=_params("parallel"),
    )(h, h, g, w_in, conv_w, conv_b)


def kernel(x, norm_mix_g, norm_ffn_g, ffn_w_up, ffn_w_down, final_norm_g, sb_w_qkv, sb_w_out, hg_w_in, hg_lower_bounds, hg_norm_g, hg_w_out, rw_mix, rw_w_in, rw_w0, rw_w1, rw_w2, rw_a0, rw_a1, rw_a2, rw_g1, rw_g2, rw_k_k, rw_k_a, rw_r_k, rw_ln_g, rw_ln_b, rw_w_out, sc_w_in, sc_conv_w, sc_conv_b, sc_w_out):
    bsz, seq, d = x.shape
    depth = norm_mix_g.shape[0]
    n_mixers = 4
    t = bsz * seq
    bf = lambda w: w.astype(BF16)
    row = lambda v: v.reshape(1, -1)
    seq3 = lambda y: y.reshape(bsz, seq, d)

    h = x.reshape(t, d)
    for i in range(depth):
        m, j = i % n_mixers, i // n_mixers
        g_mix = row(norm_mix_g[i])
        if m == 0:
            qkv = _norm_proj(h, g_mix, bf(sb_w_qkv[j]), out_dtype=BF16, tm=1024, tn=1024)
            y = _sb_attention(qkv.reshape(bsz, seq, 3 * d), n_heads=d // SB_HEAD_DIM, tq=256)
            w_o = sb_w_out[j]
        elif m == 1:
            q, lf, kf, v, gate = _hg_proj(h, g_mix, hg_lower_bounds, bf(hg_w_in[j]), layer_idx=i, tm=256)
            y = _hg_scan(seq3(q), seq3(lf), seq3(kf), seq3(v), seq3(gate), row(hg_norm_g[j]),
                         rows_per_step=512)
            w_o = hg_w_out[j]
        elif m == 2:
            parts = _rw_proj(h, g_mix, rw_mix[j], bf(rw_w_in[j, 0]), bf(rw_w_in[j, 1]), bf(rw_w_in[j, 2]),
                             row(rw_w0[j]), bf(rw_w1[j]), bf(rw_w2[j]), row(rw_a0[j]), bf(rw_a1[j]),
                             bf(rw_a2[j]), bf(rw_g1[j]), bf(rw_g2[j]), row(rw_k_k[j]), row(rw_k_a[j]),
                             seq=seq, tm=256)
            y = _rw_scan(*[seq3(p) for p in parts], row(rw_r_k[j]), row(rw_ln_g[j]), row(rw_ln_b[j]))
            w_o = rw_w_out[j]
        else:
            y = _short_conv(h, g_mix, bf(sc_w_in[j]), sc_conv_w[j], row(sc_conv_b[j]), seq=seq, tm=512)
            w_o = sc_w_out[j]
        h = _mixer_out_mlp(h, y.reshape(t, d), bf(w_o), row(norm_ffn_g[i]), bf(ffn_w_up[i]),
                           bf(ffn_w_down[i]), row(final_norm_g), final_norm=(i == depth - 1),
                           tm=1024, tf=512)
    return h.reshape(bsz, seq, d)
```

```python
import functools

import jax
import jax.numpy as jnp
from jax import lax
from jax.experimental import pallas as pl
from jax.experimental.pallas import tpu as pltpu

F32 = jnp.float32
BF16 = jnp.bfloat16

NORM_EPS = 1e-6
LOG2_E = 1.4426950408889634
RW_GN_EPS = 64e-5
RW_DECAY_OFFSET = 0.5
SB_HEAD_DIM = 64
RW_HEAD_DIM = 64
HG_HEAD_DIM = 128

V7X_LANES = 128
V7X_VMEM_LIMIT_BYTES = 56 * 1024 * 1024

EXP_ZERO_BELOW = -105.0


def _params(*sem):
    return pltpu.CompilerParams(dimension_semantics=sem, vmem_limit_bytes=V7X_VMEM_LIMIT_BYTES)


def _dot(a, b):
    return jnp.dot(a, b, preferred_element_type=F32)


def _dot_nt(a, b):
    return lax.dot_general(a, b, (((1,), (1,)), ((), ())), preferred_element_type=F32)


def _dot_tn(a, b):
    return lax.dot_general(a, b, (((0,), (0,)), ((), ())), preferred_element_type=F32)


def _split2(x):
    hi = x.astype(BF16)
    lo = (x - hi.astype(F32)).astype(BF16)
    return hi, lo


def _split3(x):
    x1 = x.astype(BF16)
    r1 = x - x1.astype(F32)
    x2 = r1.astype(BF16)
    x3 = (r1 - x2.astype(F32)).astype(BF16)
    return x1, x2, x3


def _dot_exact_lhs(m_bf16, x, dot=_dot):
    x1, x2, x3 = _split3(x)
    return dot(m_bf16, x1) + dot(m_bf16, x2) + dot(m_bf16, x3)


def _dot_exact_rhs(x, m_bf16, dot=_dot):
    hi, lo = _split2(x)
    return dot(hi, m_bf16) + dot(lo, m_bf16)


def _rms(x, g):
    return x * lax.rsqrt(jnp.mean(jnp.square(x), axis=-1, keepdims=True) + NORM_EPS) * g


def _sigmoid(x):
    return 1.0 / (1.0 + jnp.exp(-x))


def _silu(x):
    return x * _sigmoid(x)


def _softplus(x):
    return jnp.maximum(x, 0.0) + jnp.log(1.0 + jnp.exp2(jnp.abs(x) * -LOG2_E))


def _iota(shape, dim):
    return lax.broadcasted_iota(jnp.int32, shape, dim)


def _norm_proj_kernel(h_ref, g_ref, w_ref, o_ref, xn_sc):
    @pl.when(pl.program_id(1) == 0)
    def _():
        xn_sc[...] = _rms(h_ref[...], g_ref[...]).astype(BF16)

    o_ref[...] = _dot(xn_sc[...], w_ref[...]).astype(o_ref.dtype)


def _norm_proj(h, g, w, *, out_dtype, tm, tn):
    t, d = h.shape
    n = w.shape[1]
    tm, tn = min(tm, t), min(tn, n)
    return pl.pallas_call(
        _norm_proj_kernel,
        grid=(t // tm, n // tn),
        in_specs=[
            pl.BlockSpec((tm, d), lambda i, j: (i, 0)),
            pl.BlockSpec((1, d), lambda i, j: (0, 0)),
            pl.BlockSpec((d, tn), lambda i, j: (0, j)),
        ],
        out_specs=pl.BlockSpec((tm, tn), lambda i, j: (i, j)),
        out_shape=jax.ShapeDtypeStruct((t, n), out_dtype),
        scratch_shapes=[pltpu.VMEM((tm, d), BF16)],
        name="norm_proj",
        compiler_params=_params("parallel", "arbitrary"),
    )(h, g, w)


def _mlp_kernel(res_ref, y_ref, wo_ref, g_ref, wup_ref, wdn_ref, fg_ref, o_ref, xn_sc, acc_sc, *, final_norm):
    f = pl.program_id(1)

    @pl.when(f == 0)
    def _():
        h = res_ref[...] + _dot(y_ref[...].astype(BF16), wo_ref[...])
        acc_sc[...] = h
        xn_sc[...] = _rms(h, g_ref[...]).astype(BF16)

    u = _dot(xn_sc[...], wup_ref[...])
    a = jnp.square(jnp.maximum(u, 0.0)).astype(BF16)
    acc_sc[...] += _dot(a, wdn_ref[...])

    @pl.when(f == pl.num_programs(1) - 1)
    def _():
        out = acc_sc[...]
        if final_norm:
            out = _rms(out, fg_ref[...])
        o_ref[...] = out


def _mixer_out_mlp(res, y, w_o, g_ffn, w_up, w_down, final_g, *, final_norm, tm, tf):
    t, d = res.shape
    dff = w_up.shape[1]
    tm, tf = min(tm, t), min(tf, dff)
    return pl.pallas_call(
        functools.partial(_mlp_kernel, final_norm=final_norm),
        grid=(t // tm, dff // tf),
        in_specs=[
            pl.BlockSpec((tm, d), lambda i, f: (i, 0)),
            pl.BlockSpec((tm, d), lambda i, f: (i, 0)),
            pl.BlockSpec((d, d), lambda i, f: (0, 0)),
            pl.BlockSpec((1, d), lambda i, f: (0, 0)),
            pl.BlockSpec((d, tf), lambda i, f: (0, f)),
            pl.BlockSpec((tf, d), lambda i, f: (f, 0)),
            pl.BlockSpec((1, d), lambda i, f: (0, 0)),
        ],
        out_specs=pl.BlockSpec((tm, d), lambda i, f: (i, 0)),
        out_shape=jax.ShapeDtypeStruct((t, d), F32),
        scratch_shapes=[pltpu.VMEM((tm, d), BF16), pltpu.VMEM((tm, d), F32)],
        name="mixer_out_mlp",
        compiler_params=_params("parallel", "arbitrary"),
    )(res, y, w_o, g_ffn, w_up, w_down, final_g)


def _sb_kernel(q_ref, k_ref, v_ref, o_ref, acc_sc, carry_sc, *, tq, scale):
    qi = pl.program_id(2)
    n_heads = acc_sc.shape[0]
    heads = range(n_heads)
    pair_cols = [slice((hd // 2) * V7X_LANES, (hd // 2 + 1) * V7X_LANES) for hd in heads]
    lane = _iota((tq, V7X_LANES), 1)
    in_head = [lane < SB_HEAD_DIM, lane >= SB_HEAD_DIM]
    q = q_ref[0].astype(F32) * scale
    qh = [jnp.where(in_head[hd % 2], q[:, pair_cols[hd]], 0.0).astype(BF16) for hd in heads]
    row = _iota((tq, tq), 0)
    col = _iota((tq, tq), 1)
    later = (row >= col).astype(BF16)
    causal = col < row
    half = tq // 2

    acc_sc[...] = jnp.zeros_like(acc_sc)
    carry_sc[...] = jnp.zeros_like(carry_sc)

    def tile(key_start, parts):
        items = [(hd, rws, nk, mask) for rws, nk, mask in parts for hd in heads]
        kblk = k_ref[0, pl.ds(key_start, tq), :]
        vblk = v_ref[0, pl.ds(key_start, tq), :]
        z = [_dot_nt(qh[hd][rws], kblk[:nk, pair_cols[hd]]) for hd, rws, nk, _ in items]
        rest = [_softplus(zi) for zi in z]
        rest = [r if it[3] is None else jnp.where(it[3], r, 0.0) for r, it in zip(rest, items)]
        halves = [jnp.concatenate(_split2(r), axis=1) for r in rest]
        sums = [_dot(hl, jnp.concatenate([later[:it[2], :it[2]]] * 2, axis=0))
                for hl, it in zip(halves, items)]
        p = [jnp.exp(zi - s - carry_sc[it[0], it[1]]) for zi, s, it in zip(z, sums, items)]
        p = [pi if it[3] is None else jnp.where(it[3], pi, 0.0) for pi, it in zip(p, items)]
        for pi, r, (hd, rws, nk, _) in zip(p, rest, items):
            acc_sc[hd, rws] += _dot(pi.astype(BF16), vblk[:nk, pair_cols[hd]])
            carry_sc[hd, rws] += jnp.sum(r, axis=-1, keepdims=True)

    def alive():
        return jnp.min(carry_sc[...]) < -EXP_ZERO_BELOW

    tile(pl.multiple_of(qi * tq, tq),
         [(slice(0, half), half, causal[:half, :half]), (slice(half, tq), tq, causal[half:])])

    def cond(c):
        kb, go = c
        return jnp.logical_and(kb >= 0, go)

    def body(c):
        kb, _ = c
        tile(pl.multiple_of(kb * tq, tq), [(slice(0, tq), tq, None)])
        return kb - 1, alive()

    lax.while_loop(cond, body, (qi - 1, alive()))
    o_ref[0] = jnp.concatenate(
        [jnp.where(in_head[0], acc_sc[hd], acc_sc[hd + 1]) for hd in range(0, n_heads, 2)],
        axis=1).astype(o_ref.dtype)


def _sb_attention(qkv, *, n_heads, heads_per_step, tq):
    b, s, d3 = qkv.shape
    d = d3 // 3
    tq = min(tq, s)
    assert d // n_heads == SB_HEAD_DIM and 2 * SB_HEAD_DIM == V7X_LANES and heads_per_step % 2 == 0
    width = heads_per_step * SB_HEAD_DIM
    n_groups = d // width
    return pl.pallas_call(
        functools.partial(_sb_kernel, tq=tq, scale=SB_HEAD_DIM ** -0.5),
        grid=(b, n_groups, s // tq),
        in_specs=[
            pl.BlockSpec((1, tq, width), lambda bi, p, qi: (bi, qi, p)),
            pl.BlockSpec((1, s, width), lambda bi, p, qi: (bi, 0, n_groups + p)),
            pl.BlockSpec((1, s, width), lambda bi, p, qi: (bi, 0, 2 * n_groups + p)),
        ],
        out_specs=pl.BlockSpec((1, tq, width), lambda bi, p, qi: (bi, qi, p)),
        out_shape=jax.ShapeDtypeStruct((b, s, d), BF16),
        scratch_shapes=[pltpu.VMEM((heads_per_step, tq, V7X_LANES), F32),
                        pltpu.VMEM((heads_per_step, tq, 1), F32)],
        name="sb_attention",
        compiler_params=_params("parallel", "parallel", "arbitrary"),
    )(qkv, qkv, qkv)


def _hg_proj_kernel(h_ref, g_ref, lbt_ref, w_ref, q_ref, lf_ref, kf_ref, v_ref, gate_ref, *, layer_idx):
    d = h_ref.shape[1]
    xn = _rms(h_ref[...], g_ref[...]).astype(BF16)
    part = lambda j: _dot(xn, w_ref[:, j * d:(j + 1) * d])
    q_ref[...] = _silu(part(0))
    tbl = lbt_ref[...]
    e = jnp.exp(tbl - jnp.max(tbl, axis=0, keepdims=True))
    lb = jnp.sum(e[1:layer_idx + 1], axis=0, keepdims=True) / jnp.sum(e, axis=0, keepdims=True)
    sig = _sigmoid(part(1))
    lf_ref[...] = jnp.log(lb + (1.0 - lb) * sig)
    kf_ref[...] = (1.0 - lb) * (1.0 - sig)
    v_ref[...] = part(2)
    gate_ref[...] = _silu(part(3))


def _hg_proj(h, g, lb_table, w_in, *, layer_idx, tm):
    t, d = h.shape
    tm = min(tm, t)
    row = pl.BlockSpec((tm, d), lambda i: (i, 0))
    full = lambda x: pl.BlockSpec(x.shape, lambda i: (0,) * x.ndim)
    out = jax.ShapeDtypeStruct((t, d), F32)
    return pl.pallas_call(
        functools.partial(_hg_proj_kernel, layer_idx=layer_idx),
        grid=(t // tm,),
        in_specs=[row, full(g), full(lb_table), full(w_in)],
        out_specs=[row] * 5,
        out_shape=[out] * 5,
        name="hg_proj",
        compiler_params=_params("parallel"),
    )(h, g, lb_table, w_in)


HG_CHUNK = 64


def _hg_level_matrix(c_len):
    t = _iota((c_len, c_len), 0)
    j = _iota((c_len, c_len), 1)
    le = (j <= t).astype(F32)
    blocks = [le, 1.0 - le]
    m = c_len
    while m >= 2:
        mid = (t // m) * m + m // 2 - 1
        blocks.append(le - (j <= mid).astype(F32))
        m //= 2
    return jnp.concatenate(blocks, axis=0).astype(BF16)


def _hg_scan_kernel(q_ref, lf_ref, kf_ref, v_ref, gate_ref, ng_ref, o_ref, st_sc, *, n_chunks):
    c_len = HG_CHUNK

    @pl.when(pl.program_id(2) == 0)
    def _():
        st_sc[...] = jnp.zeros_like(st_sc)

    coeff = _hg_level_matrix(c_len)
    row_l = _iota((c_len, V7X_LANES), 0)
    ti = _iota((c_len, c_len), 0)
    si = _iota((c_len, c_len), 1)
    levels = []
    m = c_len
    while m >= 2:
        upper = (row_l % m) >= m // 2
        pair = jnp.logical_and(ti // m == si // m,
                               jnp.logical_and((ti % m) >= m // 2, (si % m) < m // 2))
        levels.append((upper, pair))
        m //= 2
    on_diag = ti == si

    chunks = range(n_chunks)
    rows = [slice(c * c_len, (c + 1) * c_len) for c in chunks]
    q = [q_ref[0, rw, :] for rw in rows]
    k = [kf_ref[0, rw, :] for rw in rows]
    v = [v_ref[0, rw, :].astype(BF16) for rw in rows]
    lf_parts = [_split2(lf_ref[0, rw, :]) for rw in rows]
    sums = [_dot(coeff, hi) + _dot(coeff, lo) for hi, lo in lf_parts]
    b = [s[:c_len] for s in sums]
    rest = [s[c_len:2 * c_len] for s in sums]

    scores = [jnp.where(on_diag, _dot_nt(q[c].astype(BF16), k[c].astype(BF16)), 0.0) for c in chunks]
    for lv, (upper, pair) in enumerate(levels):
        for c in chunks:
            d = sums[c][(2 + lv) * c_len:(3 + lv) * c_len]
            x = (jnp.where(upper, q[c], k[c]) * jnp.exp(-jnp.abs(d))).astype(BF16)
            scores[c] = scores[c] + jnp.where(pair, _dot_nt(x, x), 0.0)
    intra = [_dot(scores[c].astype(BF16), v[c]) for c in chunks]
    kv = [_dot_tn(v[c], (k[c] * jnp.exp(rest[c])).astype(BF16)) for c in chunks]

    states = [st_sc[...]]
    for c in chunks:
        states.append(states[c] * jnp.exp(b[c][c_len - 1:c_len]) + kv[c])
    st_sc[...] = states[n_chunks]

    for c in chunks:
        o = _dot_nt((q[c] * jnp.exp(b[c])).astype(BF16), states[c].astype(BF16)) + intra[c]
        on = o * lax.rsqrt(jnp.mean(jnp.square(o), axis=-1, keepdims=True) + NORM_EPS)
        o_ref[0, rows[c], :] = (on * ng_ref[...] * gate_ref[0, rows[c], :]).astype(o_ref.dtype)


def _hg_scan(q, lf, kf, v, gate, norm_g, *, rows_per_step):
    b, s, d = q.shape
    n_heads = d // HG_HEAD_DIM
    assert HG_HEAD_DIM == V7X_LANES
    cb = min(rows_per_step, s)
    blk = pl.BlockSpec((1, cb, V7X_LANES), lambda bi, hd, c: (bi, c, hd))
    return pl.pallas_call(
        functools.partial(_hg_scan_kernel, n_chunks=cb // HG_CHUNK),
        grid=(b, n_heads, s // cb),
        in_specs=[blk] * 5 + [pl.BlockSpec((1, V7X_LANES), lambda bi, hd, c: (0, hd))],
        out_specs=blk,
        out_shape=jax.ShapeDtypeStruct((b, s, d), BF16),
        scratch_shapes=[pltpu.VMEM((V7X_LANES, V7X_LANES), F32)],
        name="hg_scan",
        compiler_params=_params("parallel", "parallel", "arbitrary"),
    )(q, lf, kf, v, gate, norm_g)


def _head_sum_matrix(value):
    r = _iota((V7X_LANES, V7X_LANES), 0) // RW_HEAD_DIM
    c = _iota((V7X_LANES, V7X_LANES), 1) // RW_HEAD_DIM
    return jnp.where(r == c, value, 0.0).astype(BF16)


def _rw_proj_kernel(h_ref, hp_ref, g_ref, mix_ref, wr_ref, wk_ref, wv_ref, w0_ref, w1_ref, w2_ref,
                    a0_ref, a1_ref, a2_ref, g1_ref, g2_ref, kk_ref, ka_ref,
                    r_out, ld_out, k_out, v_out, kkn_out, a_out, g_out, *, tm, seq):
    i = pl.program_id(0)
    gn = g_ref[...]
    xn = _rms(h_ref[...], gn)
    prev = _rms(hp_ref[...], gn)[7:8]
    prev = jnp.where((i * tm) % seq == 0, 0.0, prev)
    shifted = jnp.where(_iota(xn.shape, 0) == 0, prev, pltpu.roll(xn, 1, 0))
    xx = shifted - xn
    lerp = lambda c: (xn + xx * mix_ref[c:c + 1]).astype(BF16)

    r = _dot(lerp(0), wr_ref[...])
    k = _dot(lerp(1), wk_ref[...])
    v = _dot(lerp(2), wv_ref[...])
    wl = w0_ref[...] + _dot(jnp.tanh(_dot(lerp(3), w1_ref[...])).astype(BF16), w2_ref[...])
    w_log = -_softplus(-wl) - RW_DECAY_OFFSET
    a = _sigmoid(a0_ref[...] + _dot(_dot(lerp(4), a1_ref[...]).astype(BF16), a2_ref[...]))
    g = _dot(_sigmoid(_dot(lerp(5), g1_ref[...])).astype(BF16), g2_ref[...])

    kk = k * kk_ref[...]
    ones = _head_sum_matrix(1.0)
    parts = []
    for p in range(kk.shape[1] // V7X_LANES):
        kp = kk[:, p * V7X_LANES:(p + 1) * V7X_LANES]
        nrm = jnp.sqrt(_dot_exact_rhs(kp * kp, ones))
        parts.append(kp / jnp.maximum(nrm, 1e-12))
    r_out[...] = r
    ld_out[...] = -jnp.exp(w_log)
    k_out[...] = k * (1.0 + (a - 1.0) * ka_ref[...])
    v_out[...] = v
    kkn_out[...] = jnp.concatenate(parts, axis=1)
    a_out[...] = a
    g_out[...] = g


def _rw_proj(h, g, mix, w_r, w_k, w_v, w0, w1, w2, a0, a1, a2, g1, g2, k_k, k_a, *, seq, tm):
    t, d = h.shape
    tm = min(tm, t)
    row = pl.BlockSpec((tm, d), lambda i: (i, 0))
    full = lambda x: pl.BlockSpec(x.shape, lambda i: (0,) * x.ndim)
    prev = pl.BlockSpec((8, d), lambda i: (jnp.maximum(i * (tm // 8) - 1, 0), 0))
    out = jax.ShapeDtypeStruct((t, d), F32)
    small = (g, mix, w_r, w_k, w_v, w0, w1, w2, a0, a1, a2, g1, g2, k_k, k_a)
    return pl.pallas_call(
        functools.partial(_rw_proj_kernel, tm=tm, seq=seq),
        grid=(t // tm,),
        in_specs=[row, prev] + [full(x) for x in small],
        out_specs=[row] * 7,
        out_shape=[out] * 7,
        name="rw_proj",
        compiler_params=_params("parallel"),
    )(h, h, *small)


RW_CHUNK = 64


def _rw_scan_kernel(r_ref, ld_ref, k_ref, v_ref, kk_ref, a_ref, g_ref, rk_ref, lng_ref, lnb_ref,
                    o_ref, st_sc):
    c_len = RW_CHUNK
    n_pairs = r_ref.shape[2] // V7X_LANES

    @pl.when(pl.program_id(1) == 0)
    def _():
        st_sc[...] = jnp.zeros_like(st_sc)

    n2 = 2 * c_len
    ri, ci = _iota((n2, n2), 0), _iota((n2, n2), 1)
    strict = ci < ri
    incl = ci <= ri
    eye = (ci == ri).astype(F32)
    tri = (_iota((c_len, c_len), 0) >= _iota((c_len, c_len), 1)).astype(BF16)
    lane = _iota((c_len, V7X_LANES), 1)
    in_head = [lane < RW_HEAD_DIM, lane >= RW_HEAD_DIM]
    head_mean = _head_sum_matrix(1.0 / RW_HEAD_DIM)
    head_sum = _head_sum_matrix(1.0)

    def stack(x):
        return jnp.concatenate([jnp.where(m, x, 0.0) for m in in_head], axis=0)

    pairs = range(n_pairs)
    chunks = range(r_ref.shape[1] // c_len)
    items = [(c, p) for c in chunks for p in pairs]
    rows = [slice(c * c_len, (c + 1) * c_len) for c in chunks]
    cols = [slice(p * V7X_LANES, (p + 1) * V7X_LANES) for p in pairs]
    ld_all = [ld_ref[0, rows[c], :] for c in chunks]
    b_all = [_dot_exact_lhs(tri, ld_all[c]) for c in chunks]

    lhs, rhs, v_st, end_rows = {}, {}, {}, {}
    for c, p in items:
        at = lambda ref: ref[0, rows[c], cols[p]]
        r, k, kk = at(r_ref), at(k_ref), at(kk_ref)
        b = b_all[c][:, cols[p]]
        beta = kk * at(a_ref)
        grow, decay, to_end = jnp.exp(-b), jnp.exp(b), jnp.exp(b[c_len - 1:c_len] - b)
        lhs[c, p] = jnp.concatenate(
            [stack(-kk * jnp.exp(b - ld_all[c][:, cols[p]])), stack(r * decay)], axis=0).astype(BF16)
        rhs[c, p] = jnp.concatenate([stack(beta * grow), stack(k * grow)], axis=0).astype(BF16)
        v_st[c, p] = stack(at(v_ref)).astype(BF16)
        end_rows[c, p] = jnp.concatenate([stack(beta * to_end), stack(k * to_end)], axis=0).astype(BF16)

    aa = {i: _dot_nt(lhs[i], rhs[i]) for i in items}
    a_ab = {i: jnp.where(strict, aa[i][:n2, :n2], 0.0) for i in items}
    a_k = {i: jnp.concatenate([jnp.where(strict, aa[i][:n2, n2:], 0.0),
                               jnp.where(incl, aa[i][n2:, n2:], 0.0)], axis=0).astype(BF16) for i in items}
    a_rb = {i: jnp.where(incl, aa[i][n2:, :n2], 0.0).astype(BF16) for i in items}
    by_v = {i: _dot(a_k[i], v_st[i]) for i in items}

    x = a_ab
    t_inv = {i: eye + x[i] for i in items}
    for _ in range(c_len.bit_length() - 2):
        xb = {i: x[i].astype(BF16) for i in items}
        x = {i: _dot(xb[i], xb[i]) for i in items}
        t_inv = {i: t_inv[i] + _dot(t_inv[i].astype(BF16), x[i].astype(BF16)) for i in items}

    st_t = [st_sc[p] for p in pairs]
    y = {}
    for c in chunks:
        by_state = [_dot_nt(lhs[c, p], st_t[p].astype(BF16)) for p in pairs]
        sa = [_dot(t_inv[c, p].astype(BF16), (by_state[p][:n2] + by_v[c, p][:n2]).astype(BF16))
              for p in pairs]
        y_st = [by_state[p][n2:] + by_v[c, p][n2:] + _dot(a_rb[c, p], sa[p].astype(BF16)) for p in pairs]
        st_t = [st_t[p] * jnp.exp(b_all[c][c_len - 1:c_len, cols[p]]) + _dot_tn(
            jnp.concatenate([sa[p].astype(BF16), v_st[c, p]], axis=0), end_rows[c, p]) for p in pairs]
        for p in pairs:
            y[c, p] = y_st[p][:c_len] + y_st[p][c_len:]
    for p in pairs:
        st_sc[p] = st_t[p]

    sums = {(c, p): _dot(jnp.concatenate(
        [y[c, p], r_ref[0, rows[c], cols[p]] * k_ref[0, rows[c], cols[p]] * rk_ref[:, cols[p]]],
        axis=0).astype(BF16), head_sum) for c, p in items}
    dlt = {i: y[i] - sums[i][:c_len] * (1.0 / RW_HEAD_DIM) for i in items}
    var = {i: _dot((dlt[i] * dlt[i]).astype(BF16), head_mean) for i in items}
    for c, p in items:
        yn = dlt[c, p] * lax.rsqrt(var[c, p] + RW_GN_EPS) * lng_ref[:, cols[p]] + lnb_ref[:, cols[p]]
        o_ref[0, rows[c], cols[p]] = ((yn + sums[c, p][c_len:] * v_ref[0, rows[c], cols[p]])
                                      * g_ref[0, rows[c], cols[p]]).astype(o_ref.dtype)


def _rw_scan(r, ld, k, v, kk, a, g, r_k, ln_g, ln_b, *, chunks_per_step):
    b, s, d = r.shape
    assert 2 * RW_HEAD_DIM == V7X_LANES
    step_rows = min(chunks_per_step * RW_CHUNK, s)
    blk = pl.BlockSpec((1, step_rows, d), lambda bi, c: (bi, c, 0))
    vec = pl.BlockSpec((1, d), lambda bi, c: (0, 0))
    return pl.pallas_call(
        _rw_scan_kernel,
        grid=(b, s // step_rows),
        in_specs=[blk] * 7 + [vec] * 3,
        out_specs=blk,
        out_shape=jax.ShapeDtypeStruct((b, s, d), BF16),
        scratch_shapes=[pltpu.VMEM((d // V7X_LANES, V7X_LANES, V7X_LANES), F32)],
        name="rw_scan",
        compiler_params=_params("parallel", "arbitrary"),
    )(r, ld, k, v, kk, a, g, r_k, ln_g, ln_b)


SC_HALO = 16


def _sc_kernel(h_ref, hp_ref, g_ref, w_ref, cw_ref, cb_ref, o_ref, *, tm, seq):
    i = pl.program_id(0)
    d = h_ref.shape[1]
    gn = g_ref[...]
    xn = _rms(h_ref[...], gn)
    xe = jnp.concatenate([_rms(hp_ref[...], gn), xn], axis=0).astype(BF16)
    gate = _dot(xn.astype(BF16), w_ref[:, 0:d])
    ch = _dot(xe, w_ref[:, d:3 * d])
    u = ch[:, :d] * ch[:, d:]
    halo_is_padding = (i * tm) % seq == 0
    u = jnp.where(jnp.logical_and(_iota(u.shape, 0) < SC_HALO, halo_is_padding), 0.0, u)
    y = (cw_ref[2:3] * u[SC_HALO:]
         + cw_ref[1:2] * pltpu.roll(u, 1, 0)[SC_HALO:]
         + cw_ref[0:1] * pltpu.roll(u, 2, 0)[SC_HALO:]
         + cb_ref[...])
    o_ref[...] = (gate * y).astype(o_ref.dtype)


def _short_conv(h, g, w_in, conv_w, conv_b, *, seq, tm):
    t, d = h.shape
    tm = min(tm, t)
    row = pl.BlockSpec((tm, d), lambda i: (i, 0))
    full = lambda x: pl.BlockSpec(x.shape, lambda i: (0,) * x.ndim)
    prev = pl.BlockSpec((SC_HALO, d), lambda i: (jnp.maximum(i * (tm // SC_HALO) - 1, 0), 0))
    return pl.pallas_call(
        functools.partial(_sc_kernel, tm=tm, seq=seq),
        grid=(t // tm,),
        in_specs=[row, prev, full(g), full(w_in), full(conv_w), full(conv_b)],
        out_specs=row,
        out_shape=jax.ShapeDtypeStruct((t, d), BF16),
        name="short_conv",
        compiler_params=_params("parallel"),
    )(h, h, g, w_in, conv_w, conv_b)


def kernel(x, norm_mix_g, norm_ffn_g, ffn_w_up, ffn_w_down, final_norm_g, sb_w_qkv, sb_w_out, hg_w_in, hg_lower_bounds, hg_norm_g, hg_w_out, rw_mix, rw_w_in, rw_w0, rw_w1, rw_w2, rw_a0, rw_a1, rw_a2, rw_g1, rw_g2, rw_k_k, rw_k_a, rw_r_k, rw_ln_g, rw_ln_b, rw_w_out, sc_w_in, sc_conv_w, sc_conv_b, sc_w_out):
    bsz, seq, d = x.shape
    depth = norm_mix_g.shape[0]
    n_mixers = 4
    t = bsz * seq
    bf = lambda w: w.astype(BF16)
    row = lambda v: v.reshape(1, -1)
    seq3 = lambda y: y.reshape(bsz, seq, d)

    h = x.reshape(t, d)
    for i in range(depth):
        m, j = i % n_mixers, i // n_mixers
        g_mix = row(norm_mix_g[i])
        if m == 0:
            qkv = _norm_proj(h, g_mix, bf(sb_w_qkv[j]), out_dtype=BF16, tm=1024, tn=1024)
            y = _sb_attention(qkv.reshape(bsz, seq, 3 * d), n_heads=d // SB_HEAD_DIM, heads_per_step=4,
                              tq=256)
            w_o = sb_w_out[j]
        elif m == 1:
            q, lf, kf, v, gate = _hg_proj(h, g_mix, hg_lower_bounds, bf(hg_w_in[j]), layer_idx=i, tm=256)
            y = _hg_scan(seq3(q), seq3(lf), seq3(kf), seq3(v), seq3(gate), row(hg_norm_g[j]),
                         rows_per_step=1024)
            w_o = hg_w_out[j]
        elif m == 2:
            parts = _rw_proj(h, g_mix, rw_mix[j], bf(rw_w_in[j, 0]), bf(rw_w_in[j, 1]), bf(rw_w_in[j, 2]),
                             row(rw_w0[j]), bf(rw_w1[j]), bf(rw_w2[j]), row(rw_a0[j]), bf(rw_a1[j]),
                             bf(rw_a2[j]), bf(rw_g1[j]), bf(rw_g2[j]), row(rw_k_k[j]), row(rw_k_a[j]),
                             seq=seq, tm=256)
            y = _rw_scan(*[seq3(p) for p in parts], row(rw_r_k[j]), row(rw_ln_g[j]), row(rw_ln_b[j]),
                         chunks_per_step=2)
            w_o = rw_w_out[j]
        else:
            y = _short_conv(h, g_mix, bf(sc_w_in[j]), sc_conv_w[j], row(sc_conv_b[j]), seq=seq, tm=512)
            w_o = sc_w_out[j]
        h = _mixer_out_mlp(h, y.reshape(t, d), bf(w_o), row(norm_ffn_g[i]), bf(ffn_w_up[i]),
                           bf(ffn_w_down[i]), row(final_norm_g), final_norm=(i == depth - 1),
                           tm=1024, tf=1024)
    return h.reshape(bsz, seq, d)
```

```python
import functools

import jax
import jax.numpy as jnp
from jax import lax
from jax.experimental import pallas as pl
from jax.experimental.pallas import tpu as pltpu

F32 = jnp.float32
BF16 = jnp.bfloat16

NORM_EPS = 1e-6
LOG2_E = 1.4426950408889634
RW_GN_EPS = 64e-5
RW_DECAY_OFFSET = 0.5
SB_HEAD_DIM = 64
RW_HEAD_DIM = 64
HG_HEAD_DIM = 128

V7X_LANES = 128
V7X_VMEM_LIMIT_BYTES = 56 * 1024 * 1024

EXP_ZERO_BELOW = -105.0

TILES = {
    "qkv_proj": dict(tm=1024, tn=1024),
    "sb_attention": dict(heads_per_step=4, tq=256),
    "hg_proj": dict(tm=256),
    "hg_scan": dict(rows_per_step=1024),
    "rw_proj": dict(tm=256),
    "rw_scan": dict(chunks_per_step=2),
    "short_conv": dict(tm=512),
    "mixer_out_mlp": dict(tm=1024, tf=1024),
}


def _params(*sem):
    return pltpu.CompilerParams(dimension_semantics=sem, vmem_limit_bytes=V7X_VMEM_LIMIT_BYTES)


def _dot(a, b):
    return jnp.dot(a, b, preferred_element_type=F32)


def _dot_nt(a, b):
    return lax.dot_general(a, b, (((1,), (1,)), ((), ())), preferred_element_type=F32)


def _dot_tn(a, b):
    return lax.dot_general(a, b, (((0,), (0,)), ((), ())), preferred_element_type=F32)


def _split2(x):
    hi = x.astype(BF16)
    lo = (x - hi.astype(F32)).astype(BF16)
    return hi, lo


def _split3(x):
    x1 = x.astype(BF16)
    r1 = x - x1.astype(F32)
    x2 = r1.astype(BF16)
    x3 = (r1 - x2.astype(F32)).astype(BF16)
    return x1, x2, x3


def _dot_exact_rhs(x, m_bf16):
    return _dot(jnp.concatenate(_split2(x), axis=1), jnp.concatenate([m_bf16, m_bf16], axis=0))


def _rms(x, g):
    return x * lax.rsqrt(jnp.mean(jnp.square(x), axis=-1, keepdims=True) + NORM_EPS) * g


def _sigmoid(x):
    return 1.0 / (1.0 + jnp.exp(-x))


def _silu(x):
    return x * _sigmoid(x)


def _softplus(x):
    return jnp.maximum(x, 0.0) + jnp.log(1.0 + jnp.exp2(jnp.abs(x) * -LOG2_E))


def _iota(shape, dim):
    return lax.broadcasted_iota(jnp.int32, shape, dim)


def _norm_proj_kernel(h_ref, g_ref, w_ref, o_ref, xn_sc):
    @pl.when(pl.program_id(1) == 0)
    def _():
        xn_sc[...] = _rms(h_ref[...], g_ref[...]).astype(BF16)

    o_ref[...] = _dot(xn_sc[...], w_ref[...]).astype(o_ref.dtype)


def _norm_proj(h, g, w, *, out_dtype, tm, tn):
    t, d = h.shape
    n = w.shape[1]
    tm, tn = min(tm, t), min(tn, n)
    return pl.pallas_call(
        _norm_proj_kernel,
        grid=(t // tm, n // tn),
        in_specs=[
            pl.BlockSpec((tm, d), lambda i, j: (i, 0)),
            pl.BlockSpec((1, d), lambda i, j: (0, 0)),
            pl.BlockSpec((d, tn), lambda i, j: (0, j)),
        ],
        out_specs=pl.BlockSpec((tm, tn), lambda i, j: (i, j)),
        out_shape=jax.ShapeDtypeStruct((t, n), out_dtype),
        scratch_shapes=[pltpu.VMEM((tm, d), BF16)],
        name="norm_proj",
        compiler_params=_params("parallel", "arbitrary"),
    )(h, g, w)


def _mlp_kernel(res_ref, y_ref, wo_ref, g_ref, wup_ref, wdn_ref, fg_ref, o_ref, xn_sc, acc_sc, *, final_norm):
    f = pl.program_id(1)

    @pl.when(f == 0)
    def _():
        h = res_ref[...] + _dot(y_ref[...].astype(BF16), wo_ref[...])
        acc_sc[...] = h
        xn_sc[...] = _rms(h, g_ref[...]).astype(BF16)

    u = _dot(xn_sc[...], wup_ref[...])
    a = jnp.square(jnp.maximum(u, 0.0)).astype(BF16)
    acc_sc[...] += _dot(a, wdn_ref[...])

    @pl.when(f == pl.num_programs(1) - 1)
    def _():
        out = acc_sc[...]
        if final_norm:
            out = _rms(out, fg_ref[...])
        o_ref[...] = out


def _mixer_out_mlp(res, y, w_o, g_ffn, w_up, w_down, final_g, *, final_norm, tm, tf):
    t, d = res.shape
    dff = w_up.shape[1]
    tm, tf = min(tm, t), min(tf, dff)
    return pl.pallas_call(
        functools.partial(_mlp_kernel, final_norm=final_norm),
        grid=(t // tm, dff // tf),
        in_specs=[
            pl.BlockSpec((tm, d), lambda i, f: (i, 0)),
            pl.BlockSpec((tm, d), lambda i, f: (i, 0)),
            pl.BlockSpec((d, d), lambda i, f: (0, 0)),
            pl.BlockSpec((1, d), lambda i, f: (0, 0)),
            pl.BlockSpec((d, tf), lambda i, f: (0, f)),
            pl.BlockSpec((tf, d), lambda i, f: (f, 0)),
            pl.BlockSpec((1, d), lambda i, f: (0, 0)),
        ],
        out_specs=pl.BlockSpec((tm, d), lambda i, f: (i, 0)),
        out_shape=jax.ShapeDtypeStruct((t, d), F32),
        scratch_shapes=[pltpu.VMEM((tm, d), BF16), pltpu.VMEM((tm, d), F32)],
        name="mixer_out_mlp",
        compiler_params=_params("parallel", "arbitrary"),
    )(res, y, w_o, g_ffn, w_up, w_down, final_g)


def _sb_kernel(q_ref, k_ref, v_ref, o_ref, acc_sc, carry_sc, *, tq, scale):
    qi = pl.program_id(2)
    n_heads = acc_sc.shape[0]
    heads = range(n_heads)
    pair_cols = [slice((hd // 2) * V7X_LANES, (hd // 2 + 1) * V7X_LANES) for hd in heads]
    lane = _iota((tq, V7X_LANES), 1)
    in_head = [lane < SB_HEAD_DIM, lane >= SB_HEAD_DIM]
    q = q_ref[0].astype(F32) * scale
    qh = [jnp.where(in_head[hd % 2], q[:, pair_cols[hd]], 0.0).astype(BF16) for hd in heads]
    row = _iota((tq, tq), 0)
    col = _iota((tq, tq), 1)
    later = (row >= col).astype(BF16)
    causal = col < row
    half = tq // 2

    acc_sc[...] = jnp.zeros_like(acc_sc)
    carry_sc[...] = jnp.zeros_like(carry_sc)

    def tile(key_start, parts):
        items = [(hd, rws, nk, mask) for rws, nk, mask in parts for hd in heads]
        kblk = k_ref[0, pl.ds(key_start, tq), :]
        vblk = v_ref[0, pl.ds(key_start, tq), :]
        z = [_dot_nt(qh[hd][rws], kblk[:nk, pair_cols[hd]]) for hd, rws, nk, _ in items]
        rest = [_softplus(zi) for zi in z]
        rest = [r if it[3] is None else jnp.where(it[3], r, 0.0) for r, it in zip(rest, items)]
        halves = [jnp.concatenate(_split2(r), axis=1) for r in rest]
        sums = [_dot(hl, jnp.concatenate([later[:it[2], :it[2]]] * 2, axis=0))
                for hl, it in zip(halves, items)]
        p = [jnp.exp(zi - s - carry_sc[it[0], it[1]]) for zi, s, it in zip(z, sums, items)]
        p = [pi if it[3] is None else jnp.where(it[3], pi, 0.0) for pi, it in zip(p, items)]
        for pi, r, (hd, rws, nk, _) in zip(p, rest, items):
            acc_sc[hd, rws] += _dot(pi.astype(BF16), vblk[:nk, pair_cols[hd]])
            carry_sc[hd, rws] += jnp.sum(r, axis=-1, keepdims=True)

    def alive():
        return jnp.min(carry_sc[...]) < -EXP_ZERO_BELOW

    tile(pl.multiple_of(qi * tq, tq),
         [(slice(0, half), half, causal[:half, :half]), (slice(half, tq), tq, causal[half:])])

    def cond(c):
        kb, go = c
        return jnp.logical_and(kb >= 0, go)

    def body(c):
        kb, _ = c
        tile(pl.multiple_of(kb * tq, tq), [(slice(0, tq), tq, None)])
        return kb - 1, alive()

    lax.while_loop(cond, body, (qi - 1, alive()))
    o_ref[0] = jnp.concatenate(
        [jnp.where(in_head[0], acc_sc[hd], acc_sc[hd + 1]) for hd in range(0, n_heads, 2)],
        axis=1).astype(o_ref.dtype)


def _sb_attention(qkv, *, n_heads, heads_per_step, tq):
    b, s, d3 = qkv.shape
    d = d3 // 3
    tq = min(tq, s)
    assert d // n_heads == SB_HEAD_DIM and 2 * SB_HEAD_DIM == V7X_LANES and heads_per_step % 2 == 0
    width = heads_per_step * SB_HEAD_DIM
    n_groups = d // width
    return pl.pallas_call(
        functools.partial(_sb_kernel, tq=tq, scale=SB_HEAD_DIM ** -0.5),
        grid=(b, n_groups, s // tq),
        in_specs=[
            pl.BlockSpec((1, tq, width), lambda bi, p, qi: (bi, qi, p)),
            pl.BlockSpec((1, s, width), lambda bi, p, qi: (bi, 0, n_groups + p)),
            pl.BlockSpec((1, s, width), lambda bi, p, qi: (bi, 0, 2 * n_groups + p)),
        ],
        out_specs=pl.BlockSpec((1, tq, width), lambda bi, p, qi: (bi, qi, p)),
        out_shape=jax.ShapeDtypeStruct((b, s, d), BF16),
        scratch_shapes=[pltpu.VMEM((heads_per_step, tq, V7X_LANES), F32),
                        pltpu.VMEM((heads_per_step, tq, 1), F32)],
        name="sb_attention",
        compiler_params=_params("parallel", "parallel", "arbitrary"),
    )(qkv, qkv, qkv)


def _hg_proj_kernel(h_ref, g_ref, lbt_ref, w_ref, q_ref, lf_ref, kf_ref, v_ref, gate_ref, *, layer_idx):
    d = h_ref.shape[1]
    xn = _rms(h_ref[...], g_ref[...]).astype(BF16)
    part = lambda j: _dot(xn, w_ref[:, j * d:(j + 1) * d])
    q_ref[...] = _silu(part(0))
    tbl = lbt_ref[...]
    e = jnp.exp(tbl - jnp.max(tbl, axis=0, keepdims=True))
    lb = jnp.sum(e[1:layer_idx + 1], axis=0, keepdims=True) / jnp.sum(e, axis=0, keepdims=True)
    sig = _sigmoid(part(1))
    lf_ref[...] = jnp.log(lb + (1.0 - lb) * sig)
    kf_ref[...] = (1.0 - lb) * (1.0 - sig)
    v_ref[...] = part(2)
    gate_ref[...] = _silu(part(3))


def _hg_proj(h, g, lb_table, w_in, *, layer_idx, tm):
    t, d = h.shape
    tm = min(tm, t)
    row = pl.BlockSpec((tm, d), lambda i: (i, 0))
    full = lambda x: pl.BlockSpec(x.shape, lambda i: (0,) * x.ndim)
    out = jax.ShapeDtypeStruct((t, d), F32)
    return pl.pallas_call(
        functools.partial(_hg_proj_kernel, layer_idx=layer_idx),
        grid=(t // tm,),
        in_specs=[row, full(g), full(lb_table), full(w_in)],
        out_specs=[row] * 5,
        out_shape=[out] * 5,
        name="hg_proj",
        compiler_params=_params("parallel"),
    )(h, g, lb_table, w_in)


HG_CHUNK = 64


def _hg_level_matrix(c_len):
    t = _iota((c_len, c_len), 0)
    j = _iota((c_len, c_len), 1)
    le = (j <= t).astype(F32)
    blocks = [le, 1.0 - le]
    m = c_len
    while m >= 2:
        mid = (t // m) * m + m // 2 - 1
        blocks.append(le - (j <= mid).astype(F32))
        m //= 2
    return jnp.concatenate(blocks, axis=0).astype(BF16)


def _hg_scan_kernel(q_ref, lf_ref, kf_ref, v_ref, gate_ref, ng_ref, o_ref, st_sc, *, n_chunks):
    c_len = HG_CHUNK

    @pl.when(pl.program_id(2) == 0)
    def _():
        st_sc[...] = jnp.zeros_like(st_sc)

    coeff = _hg_level_matrix(c_len)
    row_l = _iota((c_len, V7X_LANES), 0)
    ti = _iota((c_len, c_len), 0)
    si = _iota((c_len, c_len), 1)
    levels = []
    m = c_len
    while m >= 2:
        upper = (row_l % m) >= m // 2
        pair = jnp.logical_and(ti // m == si // m,
                               jnp.logical_and((ti % m) >= m // 2, (si % m) < m // 2))
        levels.append((upper, pair))
        m //= 2
    on_diag = ti == si

    chunks = range(n_chunks)
    rows = [slice(c * c_len, (c + 1) * c_len) for c in chunks]
    q = [q_ref[0, rw, :] for rw in rows]
    k = [kf_ref[0, rw, :] for rw in rows]
    v = [v_ref[0, rw, :].astype(BF16) for rw in rows]
    coeff2 = jnp.concatenate([coeff, coeff], axis=1)
    lf_parts = [_split2(lf_ref[0, rw, :]) for rw in rows]
    sums = []
    for c in range(0, n_chunks, 2):
        both = _dot(coeff2, jnp.concatenate(
            [jnp.concatenate([lf_parts[c][i], lf_parts[c + 1][i]], axis=1) for i in range(2)], axis=0))
        sums += [both[:, :V7X_LANES], both[:, V7X_LANES:]]
    b = [s[:c_len] for s in sums]
    rest = [s[c_len:2 * c_len] for s in sums]

    scores = [jnp.where(on_diag, _dot_nt(q[c].astype(BF16), k[c].astype(BF16)), 0.0) for c in chunks]
    for lv, (upper, pair) in enumerate(levels):
        for c in chunks:
            d = sums[c][(2 + lv) * c_len:(3 + lv) * c_len]
            x = (jnp.where(upper, q[c], k[c]) * jnp.exp(-jnp.abs(d))).astype(BF16)
            scores[c] = scores[c] + jnp.where(pair, _dot_nt(x, x), 0.0)
    intra = [_dot(scores[c].astype(BF16), v[c]) for c in chunks]
    kv = [_dot_tn(v[c], (k[c] * jnp.exp(rest[c])).astype(BF16)) for c in chunks]

    states = [st_sc[...]]
    for c in chunks:
        states.append(states[c] * jnp.exp(b[c][c_len - 1:c_len]) + kv[c])
    st_sc[...] = states[n_chunks]

    for c in chunks:
        o = _dot_nt((q[c] * jnp.exp(b[c])).astype(BF16), states[c].astype(BF16)) + intra[c]
        on = o * lax.rsqrt(jnp.mean(jnp.square(o), axis=-1, keepdims=True) + NORM_EPS)
        o_ref[0, rows[c], :] = (on * ng_ref[...] * gate_ref[0, rows[c], :]).astype(o_ref.dtype)


def _hg_scan(q, lf, kf, v, gate, norm_g, *, rows_per_step):
    b, s, d = q.shape
    n_heads = d // HG_HEAD_DIM
    assert HG_HEAD_DIM == V7X_LANES
    cb = min(rows_per_step, s)
    blk = pl.BlockSpec((1, cb, V7X_LANES), lambda bi, hd, c: (bi, c, hd))
    return pl.pallas_call(
        functools.partial(_hg_scan_kernel, n_chunks=cb // HG_CHUNK),
        grid=(b, n_heads, s // cb),
        in_specs=[blk] * 5 + [pl.BlockSpec((1, V7X_LANES), lambda bi, hd, c: (0, hd))],
        out_specs=blk,
        out_shape=jax.ShapeDtypeStruct((b, s, d), BF16),
        scratch_shapes=[pltpu.VMEM((V7X_LANES, V7X_LANES), F32)],
        name="hg_scan",
        compiler_params=_params("parallel", "parallel", "arbitrary"),
    )(q, lf, kf, v, gate, norm_g)


def _head_sum_matrix(value):
    r = _iota((V7X_LANES, V7X_LANES), 0) // RW_HEAD_DIM
    c = _iota((V7X_LANES, V7X_LANES), 1) // RW_HEAD_DIM
    return jnp.where(r == c, value, 0.0).astype(BF16)


def _rw_proj_kernel(h_ref, hp_ref, g_ref, mix_ref, wr_ref, wk_ref, wv_ref, w0_ref, w1_ref, w2_ref,
                    a0_ref, a1_ref, a2_ref, g1_ref, g2_ref, kk_ref, ka_ref,
                    r_out, ld_out, k_out, v_out, kkn_out, a_out, g_out, *, tm, seq):
    i = pl.program_id(0)
    gn = g_ref[...]
    xn = _rms(h_ref[...], gn)
    prev = _rms(hp_ref[...], gn)[7:8]
    prev = jnp.where((i * tm) % seq == 0, 0.0, prev)
    shifted = jnp.where(_iota(xn.shape, 0) == 0, prev, pltpu.roll(xn, 1, 0))
    xx = shifted - xn
    lerp = lambda c: (xn + xx * mix_ref[c:c + 1]).astype(BF16)

    r = _dot(lerp(0), wr_ref[...])
    k = _dot(lerp(1), wk_ref[...])
    v = _dot(lerp(2), wv_ref[...])
    wl = w0_ref[...] + _dot(jnp.tanh(_dot(lerp(3), w1_ref[...])).astype(BF16), w2_ref[...])
    w_log = -_softplus(-wl) - RW_DECAY_OFFSET
    a = _sigmoid(a0_ref[...] + _dot(_dot(lerp(4), a1_ref[...]).astype(BF16), a2_ref[...]))
    g = _dot(_sigmoid(_dot(lerp(5), g1_ref[...])).astype(BF16), g2_ref[...])

    kk = k * kk_ref[...]
    ones = _head_sum_matrix(1.0)
    parts = []
    for p in range(kk.shape[1] // V7X_LANES):
        kp = kk[:, p * V7X_LANES:(p + 1) * V7X_LANES]
        parts.append(kp * lax.rsqrt(jnp.maximum(_dot_exact_rhs(kp * kp, ones), 1e-24)))
    r_out[...] = r
    ld_out[...] = -jnp.exp(w_log)
    k_out[...] = k * (1.0 + (a - 1.0) * ka_ref[...])
    v_out[...] = v
    kkn_out[...] = jnp.concatenate(parts, axis=1)
    a_out[...] = a
    g_out[...] = g


def _rw_proj(h, g, mix, w_r, w_k, w_v, w0, w1, w2, a0, a1, a2, g1, g2, k_k, k_a, *, seq, tm):
    t, d = h.shape
    tm = min(tm, t)
    row = pl.BlockSpec((tm, d), lambda i: (i, 0))
    full = lambda x: pl.BlockSpec(x.shape, lambda i: (0,) * x.ndim)
    prev = pl.BlockSpec((8, d), lambda i: (jnp.maximum(i * (tm // 8) - 1, 0), 0))
    out = jax.ShapeDtypeStruct((t, d), F32)
    small = (g, mix, w_r, w_k, w_v, w0, w1, w2, a0, a1, a2, g1, g2, k_k, k_a)
    return pl.pallas_call(
        functools.partial(_rw_proj_kernel, tm=tm, seq=seq),
        grid=(t // tm,),
        in_specs=[row, prev] + [full(x) for x in small],
        out_specs=[row] * 7,
        out_shape=[out] * 7,
        name="rw_proj",
        compiler_params=_params("parallel"),
    )(h, h, *small)


RW_CHUNK = 64


def _rw_scan_kernel(r_ref, ld_ref, k_ref, v_ref, kk_ref, a_ref, g_ref, rk_ref, lng_ref, lnb_ref,
                    o_ref, st_sc):
    c_len = RW_CHUNK
    n_pairs = r_ref.shape[2] // V7X_LANES

    @pl.when(pl.program_id(1) == 0)
    def _():
        st_sc[...] = jnp.zeros_like(st_sc)

    n2 = 2 * c_len
    ri, ci = _iota((n2, n2), 0), _iota((n2, n2), 1)
    strict = ci < ri
    incl = ci <= ri
    eye = (ci == ri).astype(F32)
    tri = (_iota((c_len, c_len), 0) >= _iota((c_len, c_len), 1)).astype(BF16)
    lane = _iota((c_len, V7X_LANES), 1)
    in_head = [lane < RW_HEAD_DIM, lane >= RW_HEAD_DIM]
    head_mean = _head_sum_matrix(1.0 / RW_HEAD_DIM)
    head_sum = _head_sum_matrix(1.0)

    def stack(x):
        return jnp.concatenate([jnp.where(m, x, 0.0) for m in in_head], axis=0)

    pairs = range(n_pairs)
    chunks = range(r_ref.shape[1] // c_len)
    items = [(c, p) for c in chunks for p in pairs]
    rows = [slice(c * c_len, (c + 1) * c_len) for c in chunks]
    cols = [slice(p * V7X_LANES, (p + 1) * V7X_LANES) for p in pairs]
    ld_all = [ld_ref[0, rows[c], :] for c in chunks]
    tri3 = jnp.concatenate([tri, tri, tri], axis=1)
    b_all = [_dot(tri3, jnp.concatenate(_split3(ld_all[c]), axis=0)) for c in chunks]

    lhs, rhs, v_st, end_rows = {}, {}, {}, {}
    for c, p in items:
        at = lambda ref: ref[0, rows[c], cols[p]]
        r, k, kk = at(r_ref), at(k_ref), at(kk_ref)
        b = b_all[c][:, cols[p]]
        beta = kk * at(a_ref)
        grow, decay, to_end = jnp.exp(-b), jnp.exp(b), jnp.exp(b[c_len - 1:c_len] - b)
        lhs[c, p] = jnp.concatenate(
            [stack(-kk * jnp.exp(b - ld_all[c][:, cols[p]])), stack(r * decay)], axis=0).astype(BF16)
        rhs[c, p] = jnp.concatenate([stack(beta * grow), stack(k * grow)], axis=0).astype(BF16)
        v_st[c, p] = stack(at(v_ref)).astype(BF16)
        end_rows[c, p] = jnp.concatenate([stack(beta * to_end), stack(k * to_end)], axis=0).astype(BF16)

    aa = {i: _dot_nt(lhs[i], rhs[i]) for i in items}
    a_ab = {i: jnp.where(strict, aa[i][:n2, :n2], 0.0) for i in items}
    a_k = {i: jnp.concatenate([jnp.where(strict, aa[i][:n2, n2:], 0.0),
                               jnp.where(incl, aa[i][n2:, n2:], 0.0)], axis=0).astype(BF16) for i in items}
    a_rb = {i: jnp.where(incl, aa[i][n2:, :n2], 0.0).astype(BF16) for i in items}
    by_v = {i: _dot(a_k[i], v_st[i]) for i in items}

    t_inv = {i: eye + a_ab[i] for i in items}
    xb = {i: a_ab[i].astype(BF16) for i in items}
    xb = {i: _dot(xb[i], xb[i]).astype(BF16) for i in items}
    for _ in range(c_len.bit_length() - 3):
        both = {i: _dot(jnp.concatenate([xb[i], t_inv[i].astype(BF16)], axis=0), xb[i]) for i in items}
        t_inv = {i: t_inv[i] + both[i][n2:] for i in items}
        xb = {i: both[i][:n2].astype(BF16) for i in items}
    t_inv = {i: t_inv[i] + _dot(t_inv[i].astype(BF16), xb[i]) for i in items}

    st_t = [st_sc[p] for p in pairs]
    y = {}
    for c in chunks:
        by_state = [_dot_nt(lhs[c, p], st_t[p].astype(BF16)) for p in pairs]
        sa = [_dot(t_inv[c, p].astype(BF16), (by_state[p][:n2] + by_v[c, p][:n2]).astype(BF16))
              for p in pairs]
        y_st = [by_state[p][n2:] + by_v[c, p][n2:] + _dot(a_rb[c, p], sa[p].astype(BF16)) for p in pairs]
        st_t = [st_t[p] * jnp.exp(b_all[c][c_len - 1:c_len, cols[p]]) + _dot_tn(
            jnp.concatenate([sa[p].astype(BF16), v_st[c, p]], axis=0), end_rows[c, p]) for p in pairs]
        for p in pairs:
            y[c, p] = y_st[p][:c_len] + y_st[p][c_len:]
    for p in pairs:
        st_sc[p] = st_t[p]

    sums = {(c, p): _dot(jnp.concatenate(
        [y[c, p], r_ref[0, rows[c], cols[p]] * k_ref[0, rows[c], cols[p]] * rk_ref[:, cols[p]]],
        axis=0).astype(BF16), head_sum) for c, p in items}
    dlt = {i: y[i] - sums[i][:c_len] * (1.0 / RW_HEAD_DIM) for i in items}
    var = {i: _dot((dlt[i] * dlt[i]).astype(BF16), head_mean) for i in items}
    for c, p in items:
        yn = dlt[c, p] * lax.rsqrt(var[c, p] + RW_GN_EPS) * lng_ref[:, cols[p]] + lnb_ref[:, cols[p]]
        o_ref[0, rows[c], cols[p]] = ((yn + sums[c, p][c_len:] * v_ref[0, rows[c], cols[p]])
                                      * g_ref[0, rows[c], cols[p]]).astype(o_ref.dtype)


def _rw_scan(r, ld, k, v, kk, a, g, r_k, ln_g, ln_b, *, chunks_per_step):
    b, s, d = r.shape
    assert 2 * RW_HEAD_DIM == V7X_LANES
    step_rows = min(chunks_per_step * RW_CHUNK, s)
    blk = pl.BlockSpec((1, step_rows, d), lambda bi, c: (bi, c, 0))
    vec = pl.BlockSpec((1, d), lambda bi, c: (0, 0))
    return pl.pallas_call(
        _rw_scan_kernel,
        grid=(b, s // step_rows),
        in_specs=[blk] * 7 + [vec] * 3,
        out_specs=blk,
        out_shape=jax.ShapeDtypeStruct((b, s, d), BF16),
        scratch_shapes=[pltpu.VMEM((d // V7X_LANES, V7X_LANES, V7X_LANES), F32)],
        name="rw_scan",
        compiler_params=_params("parallel", "arbitrary"),
    )(r, ld, k, v, kk, a, g, r_k, ln_g, ln_b)


SC_HALO = 16


def _sc_kernel(h_ref, hp_ref, g_ref, w_ref, cw_ref, cb_ref, o_ref, *, tm, seq):
    i = pl.program_id(0)
    d = h_ref.shape[1]
    gn = g_ref[...]
    xn = _rms(h_ref[...], gn)
    xe = jnp.concatenate([_rms(hp_ref[...], gn), xn], axis=0).astype(BF16)
    gate = _dot(xn.astype(BF16), w_ref[:, 0:d])
    ch = _dot(xe, w_ref[:, d:3 * d])
    u = ch[:, :d] * ch[:, d:]
    halo_is_padding = (i * tm) % seq == 0
    u = jnp.where(jnp.logical_and(_iota(u.shape, 0) < SC_HALO, halo_is_padding), 0.0, u)
    y = (cw_ref[2:3] * u[SC_HALO:]
         + cw_ref[1:2] * pltpu.roll(u, 1, 0)[SC_HALO:]
         + cw_ref[0:1] * pltpu.roll(u, 2, 0)[SC_HALO:]
         + cb_ref[...])
    o_ref[...] = (gate * y).astype(o_ref.dtype)


def _short_conv(h, g, w_in, conv_w, conv_b, *, seq, tm):
    t, d = h.shape
    tm = min(tm, t)
    row = pl.BlockSpec((tm, d), lambda i: (i, 0))
    full = lambda x: pl.BlockSpec(x.shape, lambda i: (0,) * x.ndim)
    prev = pl.BlockSpec((SC_HALO, d), lambda i: (jnp.maximum(i * (tm // SC_HALO) - 1, 0), 0))
    return pl.pallas_call(
        functools.partial(_sc_kernel, tm=tm, seq=seq),
        grid=(t // tm,),
        in_specs=[row, prev, full(g), full(w_in), full(conv_w), full(conv_b)],
        out_specs=row,
        out_shape=jax.ShapeDtypeStruct((t, d), BF16),
        name="short_conv",
        compiler_params=_params("parallel"),
    )(h, h, g, w_in, conv_w, conv_b)


def kernel(x, norm_mix_g, norm_ffn_g, ffn_w_up, ffn_w_down, final_norm_g, sb_w_qkv, sb_w_out, hg_w_in, hg_lower_bounds, hg_norm_g, hg_w_out, rw_mix, rw_w_in, rw_w0, rw_w1, rw_w2, rw_a0, rw_a1, rw_a2, rw_g1, rw_g2, rw_k_k, rw_k_a, rw_r_k, rw_ln_g, rw_ln_b, rw_w_out, sc_w_in, sc_conv_w, sc_conv_b, sc_w_out):
    bsz, seq, d = x.shape
    depth = norm_mix_g.shape[0]
    n_mixers = 4
    t = bsz * seq
    bf = lambda w: w.astype(BF16)
    row = lambda v: v.reshape(1, -1)
    seq3 = lambda y: y.reshape(bsz, seq, d)

    h = x.reshape(t, d)
    for i in range(depth):
        m, j = i % n_mixers, i // n_mixers
        g_mix = row(norm_mix_g[i])
        if m == 0:
            qkv = _norm_proj(h, g_mix, bf(sb_w_qkv[j]), out_dtype=BF16, **TILES["qkv_proj"])
            y = _sb_attention(qkv.reshape(bsz, seq, 3 * d), n_heads=d // SB_HEAD_DIM, **TILES["sb_attention"])
            w_o = sb_w_out[j]
        elif m == 1:
            q, lf, kf, v, gate = _hg_proj(h, g_mix, hg_lower_bounds, bf(hg_w_in[j]), layer_idx=i,
                                          **TILES["hg_proj"])
            y = _hg_scan(seq3(q), seq3(lf), seq3(kf), seq3(v), seq3(gate), row(hg_norm_g[j]),
                         **TILES["hg_scan"])
            w_o = hg_w_out[j]
        elif m == 2:
            parts = _rw_proj(h, g_mix, rw_mix[j], bf(rw_w_in[j, 0]), bf(rw_w_in[j, 1]), bf(rw_w_in[j, 2]),
                             row(rw_w0[j]), bf(rw_w1[j]), bf(rw_w2[j]), row(rw_a0[j]), bf(rw_a1[j]),
                             bf(rw_a2[j]), bf(rw_g1[j]), bf(rw_g2[j]), row(rw_k_k[j]), row(rw_k_a[j]),
                             seq=seq, **TILES["rw_proj"])
            y = _rw_scan(*[seq3(p) for p in parts], row(rw_r_k[j]), row(rw_ln_g[j]), row(rw_ln_b[j]),
                         **TILES["rw_scan"])
            w_o = rw_w_out[j]
        else:
            y = _short_conv(h, g_mix, bf(sc_w_in[j]), sc_conv_w[j], row(sc_conv_b[j]), seq=seq,
                            **TILES["short_conv"])
            w_o = sc_w_out[j]
        h = _mixer_out_mlp(h, y.reshape(t, d), bf(w_o), row(norm_ffn_g[i]), bf(ffn_w_up[i]),
                           bf(ffn_w_down[i]), row(final_norm_g), final_norm=(i == depth - 1),
                           **TILES["mixer_out_mlp"])
    return h.reshape(bsz, seq, d)
```

```python
import functools

import jax
import jax.numpy as jnp
from jax import lax
from jax.experimental import pallas as pl
from jax.experimental.pallas import tpu as pltpu

F32 = jnp.float32
BF16 = jnp.bfloat16

NORM_EPS = 1e-6
LOG2_E = 1.4426950408889634
RW_GN_EPS = 64e-5
RW_DECAY_OFFSET = 0.5
SB_HEAD_DIM = 64
RW_HEAD_DIM = 64
HG_HEAD_DIM = 128

V7X_LANES = 128
V7X_VMEM_LIMIT_BYTES = 56 * 1024 * 1024

EXP_ZERO_BELOW = -105.0

TILES = {
    "qkv_proj": dict(tm=1024, tn=1024),
    "sb_attention": dict(heads_per_step=8, tq=256),
    "hg_proj": dict(tm=256),
    "hg_scan": dict(rows_per_step=1024),
    "rw_proj": dict(tm=256),
    "rw_scan": dict(chunks_per_step=4),
    "short_conv": dict(tm=512),
    "mixer_out_mlp": dict(tm=1024, tf=1024),
}


def _params(*sem):
    return pltpu.CompilerParams(dimension_semantics=sem, vmem_limit_bytes=V7X_VMEM_LIMIT_BYTES)


def _dot(a, b):
    return jnp.dot(a, b, preferred_element_type=F32)


def _dot_nt(a, b):
    return lax.dot_general(a, b, (((1,), (1,)), ((), ())), preferred_element_type=F32)


def _dot_tn(a, b):
    return lax.dot_general(a, b, (((0,), (0,)), ((), ())), preferred_element_type=F32)


def _split2(x):
    hi = x.astype(BF16)
    lo = (x - hi.astype(F32)).astype(BF16)
    return hi, lo


def _split3(x):
    x1 = x.astype(BF16)
    r1 = x - x1.astype(F32)
    x2 = r1.astype(BF16)
    x3 = (r1 - x2.astype(F32)).astype(BF16)
    return x1, x2, x3


def _dot_exact_rhs(x, m_bf16):
    return _dot(jnp.concatenate(_split2(x), axis=1), jnp.concatenate([m_bf16, m_bf16], axis=0))


def _rms(x, g):
    return x * lax.rsqrt(jnp.mean(jnp.square(x), axis=-1, keepdims=True) + NORM_EPS) * g


def _sigmoid(x):
    return 1.0 / (1.0 + jnp.exp(-x))


def _silu(x):
    return x * _sigmoid(x)


def _softplus(x):
    return jnp.maximum(x, 0.0) + jnp.log(1.0 + jnp.exp2(jnp.abs(x) * -LOG2_E))


def _iota(shape, dim):
    return lax.broadcasted_iota(jnp.int32, shape, dim)


def _norm_proj_kernel(h_ref, g_ref, w_ref, o_ref, xn_sc):
    @pl.when(pl.program_id(1) == 0)
    def _():
        xn_sc[...] = _rms(h_ref[...], g_ref[...]).astype(BF16)

    o_ref[...] = _dot(xn_sc[...], w_ref[...]).astype(o_ref.dtype)


def _norm_proj(h, g, w, *, out_dtype, tm, tn):
    t, d = h.shape
    n = w.shape[1]
    tm, tn = min(tm, t), min(tn, n)
    return pl.pallas_call(
        _norm_proj_kernel,
        grid=(t // tm, n // tn),
        in_specs=[
            pl.BlockSpec((tm, d), lambda i, j: (i, 0)),
            pl.BlockSpec((1, d), lambda i, j: (0, 0)),
            pl.BlockSpec((d, tn), lambda i, j: (0, j)),
        ],
        out_specs=pl.BlockSpec((tm, tn), lambda i, j: (i, j)),
        out_shape=jax.ShapeDtypeStruct((t, n), out_dtype),
        scratch_shapes=[pltpu.VMEM((tm, d), BF16)],
        name="norm_proj",
        compiler_params=_params("parallel", "arbitrary"),
    )(h, g, w)


def _mlp_kernel(res_ref, y_ref, wo_ref, g_ref, wup_ref, wdn_ref, fg_ref, o_ref, xn_sc, acc_sc, *, final_norm):
    f = pl.program_id(1)

    @pl.when(f == 0)
    def _():
        h = res_ref[...] + _dot(y_ref[...].astype(BF16), wo_ref[...])
        acc_sc[...] = h
        xn_sc[...] = _rms(h, g_ref[...]).astype(BF16)

    u = _dot(xn_sc[...], wup_ref[...])
    a = jnp.square(jnp.maximum(u, 0.0)).astype(BF16)
    acc_sc[...] += _dot(a, wdn_ref[...])

    @pl.when(f == pl.num_programs(1) - 1)
    def _():
        out = acc_sc[...]
        if final_norm:
            out = _rms(out, fg_ref[...])
        o_ref[...] = out


def _mixer_out_mlp(res, y, w_o, g_ffn, w_up, w_down, final_g, *, final_norm, tm, tf):
    t, d = res.shape
    dff = w_up.shape[1]
    tm, tf = min(tm, t), min(tf, dff)
    return pl.pallas_call(
        functools.partial(_mlp_kernel, final_norm=final_norm),
        grid=(t // tm, dff // tf),
        in_specs=[
            pl.BlockSpec((tm, d), lambda i, f: (i, 0)),
            pl.BlockSpec((tm, d), lambda i, f: (i, 0)),
            pl.BlockSpec((d, d), lambda i, f: (0, 0)),
            pl.BlockSpec((1, d), lambda i, f: (0, 0)),
            pl.BlockSpec((d, tf), lambda i, f: (0, f)),
            pl.BlockSpec((tf, d), lambda i, f: (f, 0)),
            pl.BlockSpec((1, d), lambda i, f: (0, 0)),
        ],
        out_specs=pl.BlockSpec((tm, d), lambda i, f: (i, 0)),
        out_shape=jax.ShapeDtypeStruct((t, d), F32),
        scratch_shapes=[pltpu.VMEM((tm, d), BF16), pltpu.VMEM((tm, d), F32)],
        name="mixer_out_mlp",
        compiler_params=_params("parallel", "arbitrary"),
    )(res, y, w_o, g_ffn, w_up, w_down, final_g)


def _sb_kernel(q_ref, k_ref, v_ref, o_ref, acc_sc, carry_sc, *, tq, scale):
    qi = pl.program_id(2)
    n_heads = acc_sc.shape[0]
    heads = range(n_heads)
    pair_cols = [slice((hd // 2) * V7X_LANES, (hd // 2 + 1) * V7X_LANES) for hd in heads]
    lane = _iota((tq, V7X_LANES), 1)
    in_head = [lane < SB_HEAD_DIM, lane >= SB_HEAD_DIM]
    q = q_ref[0].astype(F32) * scale
    qh = [jnp.where(in_head[hd % 2], q[:, pair_cols[hd]], 0.0).astype(BF16) for hd in heads]
    row = _iota((tq, tq), 0)
    col = _iota((tq, tq), 1)
    later = (row >= col).astype(BF16)
    causal = col < row
    half = tq // 2

    acc_sc[...] = jnp.zeros_like(acc_sc)
    carry_sc[...] = jnp.zeros_like(carry_sc)

    def tile(key_start, parts):
        items = [(hd, rws, nk, mask) for rws, nk, mask in parts for hd in heads]
        kblk = k_ref[0, pl.ds(key_start, tq), :]
        vblk = v_ref[0, pl.ds(key_start, tq), :]
        z = [_dot_nt(qh[hd][rws], kblk[:nk, pair_cols[hd]]) for hd, rws, nk, _ in items]
        rest = [_softplus(zi) for zi in z]
        rest = [r if it[3] is None else jnp.where(it[3], r, 0.0) for r, it in zip(rest, items)]
        halves = [jnp.concatenate(_split2(r), axis=1) for r in rest]
        sums = [_dot(hl, jnp.concatenate([later[:it[2], :it[2]]] * 2, axis=0))
                for hl, it in zip(halves, items)]
        p = [jnp.exp(zi - s - carry_sc[it[0], it[1]]) for zi, s, it in zip(z, sums, items)]
        p = [pi if it[3] is None else jnp.where(it[3], pi, 0.0) for pi, it in zip(p, items)]
        for pi, r, (hd, rws, nk, _) in zip(p, rest, items):
            acc_sc[hd, rws] += _dot(pi.astype(BF16), vblk[:nk, pair_cols[hd]])
            carry_sc[hd, rws] += jnp.sum(r, axis=-1, keepdims=True)

    def alive():
        return jnp.min(carry_sc[...]) < -EXP_ZERO_BELOW

    tile(pl.multiple_of(qi * tq, tq),
         [(slice(0, half), half, causal[:half, :half]), (slice(half, tq), tq, causal[half:])])

    def cond(c):
        kb, go = c
        return jnp.logical_and(kb >= 0, go)

    def body(c):
        kb, _ = c
        tile(pl.multiple_of(kb * tq, tq), [(slice(0, tq), tq, None)])
        return kb - 1, alive()

    lax.while_loop(cond, body, (qi - 1, alive()))
    o_ref[0] = jnp.concatenate(
        [jnp.where(in_head[0], acc_sc[hd], acc_sc[hd + 1]) for hd in range(0, n_heads, 2)],
        axis=1).astype(o_ref.dtype)


def _sb_attention(qkv, *, n_heads, heads_per_step, tq):
    b, s, d3 = qkv.shape
    d = d3 // 3
    tq = min(tq, s)
    assert d // n_heads == SB_HEAD_DIM and 2 * SB_HEAD_DIM == V7X_LANES and heads_per_step % 2 == 0
    width = heads_per_step * SB_HEAD_DIM
    n_groups = d // width
    return pl.pallas_call(
        functools.partial(_sb_kernel, tq=tq, scale=SB_HEAD_DIM ** -0.5),
        grid=(b, n_groups, s // tq),
        in_specs=[
            pl.BlockSpec((1, tq, width), lambda bi, p, qi: (bi, qi, p)),
            pl.BlockSpec((1, s, width), lambda bi, p, qi: (bi, 0, n_groups + p)),
            pl.BlockSpec((1, s, width), lambda bi, p, qi: (bi, 0, 2 * n_groups + p)),
        ],
        out_specs=pl.BlockSpec((1, tq, width), lambda bi, p, qi: (bi, qi, p)),
        out_shape=jax.ShapeDtypeStruct((b, s, d), BF16),
        scratch_shapes=[pltpu.VMEM((heads_per_step, tq, V7X_LANES), F32),
                        pltpu.VMEM((heads_per_step, tq, 1), F32)],
        name="sb_attention",
        compiler_params=_params("parallel", "parallel", "arbitrary"),
    )(qkv, qkv, qkv)


def _hg_proj_kernel(h_ref, g_ref, lbt_ref, w_ref, q_ref, lf_ref, kf_ref, v_ref, gate_ref, *, layer_idx):
    d = h_ref.shape[1]
    xn = _rms(h_ref[...], g_ref[...]).astype(BF16)
    part = lambda j: _dot(xn, w_ref[:, j * d:(j + 1) * d])
    q_ref[...] = _silu(part(0))
    tbl = lbt_ref[...]
    e = jnp.exp(tbl - jnp.max(tbl, axis=0, keepdims=True))
    lb = jnp.sum(e[1:layer_idx + 1], axis=0, keepdims=True) / jnp.sum(e, axis=0, keepdims=True)
    sig = _sigmoid(part(1))
    lf_ref[...] = jnp.log(lb + (1.0 - lb) * sig)
    kf_ref[...] = (1.0 - lb) * (1.0 - sig)
    v_ref[...] = part(2)
    gate_ref[...] = _silu(part(3))


def _hg_proj(h, g, lb_table, w_in, *, layer_idx, tm):
    t, d = h.shape
    tm = min(tm, t)
    row = pl.BlockSpec((tm, d), lambda i: (i, 0))
    full = lambda x: pl.BlockSpec(x.shape, lambda i: (0,) * x.ndim)
    out = jax.ShapeDtypeStruct((t, d), F32)
    return pl.pallas_call(
        functools.partial(_hg_proj_kernel, layer_idx=layer_idx),
        grid=(t // tm,),
        in_specs=[row, full(g), full(lb_table), full(w_in)],
        out_specs=[row] * 5,
        out_shape=[out] * 5,
        name="hg_proj",
        compiler_params=_params("parallel"),
    )(h, g, lb_table, w_in)


HG_CHUNK = 64


def _hg_level_matrix(c_len):
    t = _iota((c_len, c_len), 0)
    j = _iota((c_len, c_len), 1)
    le = (j <= t).astype(F32)
    blocks = [le, 1.0 - le]
    m = c_len
    while m >= 2:
        mid = (t // m) * m + m // 2 - 1
        blocks.append(le - (j <= mid).astype(F32))
        m //= 2
    return jnp.concatenate(blocks, axis=0).astype(BF16)


def _hg_scan_kernel(q_ref, lf_ref, kf_ref, v_ref, gate_ref, ng_ref, o_ref, st_sc, *, n_chunks):
    c_len = HG_CHUNK

    @pl.when(pl.program_id(2) == 0)
    def _():
        st_sc[...] = jnp.zeros_like(st_sc)

    coeff = _hg_level_matrix(c_len)
    row_l = _iota((c_len, V7X_LANES), 0)
    ti = _iota((c_len, c_len), 0)
    si = _iota((c_len, c_len), 1)
    levels = []
    m = c_len
    while m >= 2:
        upper = (row_l % m) >= m // 2
        pair = jnp.logical_and(ti // m == si // m,
                               jnp.logical_and((ti % m) >= m // 2, (si % m) < m // 2))
        levels.append((upper, pair))
        m //= 2
    on_diag = ti == si

    chunks = range(n_chunks)
    rows = [slice(c * c_len, (c + 1) * c_len) for c in chunks]
    q = [q_ref[0, rw, :] for rw in rows]
    k = [kf_ref[0, rw, :] for rw in rows]
    v = [v_ref[0, rw, :].astype(BF16) for rw in rows]
    coeff2 = jnp.concatenate([coeff, coeff], axis=1)
    lf_parts = [_split2(lf_ref[0, rw, :]) for rw in rows]
    sums = []
    for c in range(0, n_chunks, 2):
        both = _dot(coeff2, jnp.concatenate(
            [jnp.concatenate([lf_parts[c][i], lf_parts[c + 1][i]], axis=1) for i in range(2)], axis=0))
        sums += [both[:, :V7X_LANES], both[:, V7X_LANES:]]
    b = [s[:c_len] for s in sums]
    rest = [s[c_len:2 * c_len] for s in sums]

    scores = [jnp.where(on_diag, _dot_nt(q[c].astype(BF16), k[c].astype(BF16)), 0.0) for c in chunks]
    for lv, (upper, pair) in enumerate(levels):
        for c in chunks:
            d = sums[c][(2 + lv) * c_len:(3 + lv) * c_len]
            x = (jnp.where(upper, q[c], k[c]) * jnp.exp(-jnp.abs(d))).astype(BF16)
            scores[c] = scores[c] + jnp.where(pair, _dot_nt(x, x), 0.0)
    intra = [_dot(scores[c].astype(BF16), v[c]) for c in chunks]
    kv = [_dot_tn(v[c], (k[c] * jnp.exp(rest[c])).astype(BF16)) for c in chunks]

    states = [st_sc[...]]
    for c in chunks:
        states.append(states[c] * jnp.exp(b[c][c_len - 1:c_len]) + kv[c])
    st_sc[...] = states[n_chunks]

    for c in chunks:
        o = _dot_nt((q[c] * jnp.exp(b[c])).astype(BF16), states[c].astype(BF16)) + intra[c]
        on = o * lax.rsqrt(jnp.mean(jnp.square(o), axis=-1, keepdims=True) + NORM_EPS)
        o_ref[0, rows[c], :] = (on * ng_ref[...] * gate_ref[0, rows[c], :]).astype(o_ref.dtype)


def _hg_scan(q, lf, kf, v, gate, norm_g, *, rows_per_step):
    b, s, d = q.shape
    n_heads = d // HG_HEAD_DIM
    assert HG_HEAD_DIM == V7X_LANES
    cb = min(rows_per_step, s)
    blk = pl.BlockSpec((1, cb, V7X_LANES), lambda bi, hd, c: (bi, c, hd))
    return pl.pallas_call(
        functools.partial(_hg_scan_kernel, n_chunks=cb // HG_CHUNK),
        grid=(b, n_heads, s // cb),
        in_specs=[blk] * 5 + [pl.BlockSpec((1, V7X_LANES), lambda bi, hd, c: (0, hd))],
        out_specs=blk,
        out_shape=jax.ShapeDtypeStruct((b, s, d), BF16),
        scratch_shapes=[pltpu.VMEM((V7X_LANES, V7X_LANES), F32)],
        name="hg_scan",
        compiler_params=_params("parallel", "parallel", "arbitrary"),
    )(q, lf, kf, v, gate, norm_g)


def _head_sum_matrix(value):
    r = _iota((V7X_LANES, V7X_LANES), 0) // RW_HEAD_DIM
    c = _iota((V7X_LANES, V7X_LANES), 1) // RW_HEAD_DIM
    return jnp.where(r == c, value, 0.0).astype(BF16)


def _rw_proj_kernel(h_ref, hp_ref, g_ref, mix_ref, wr_ref, wk_ref, wv_ref, w0_ref, w1_ref, w2_ref,
                    a0_ref, a1_ref, a2_ref, g1_ref, g2_ref, kk_ref, ka_ref,
                    r_out, ld_out, k_out, v_out, kkn_out, a_out, g_out, *, tm, seq):
    i = pl.program_id(0)
    gn = g_ref[...]
    xn = _rms(h_ref[...], gn)
    prev = _rms(hp_ref[...], gn)[7:8]
    prev = jnp.where((i * tm) % seq == 0, 0.0, prev)
    shifted = jnp.where(_iota(xn.shape, 0) == 0, prev, pltpu.roll(xn, 1, 0))
    xx = shifted - xn
    lerp = lambda c: (xn + xx * mix_ref[c:c + 1]).astype(BF16)

    r = _dot(lerp(0), wr_ref[...])
    k = _dot(lerp(1), wk_ref[...])
    v = _dot(lerp(2), wv_ref[...])
    wl = w0_ref[...] + _dot(jnp.tanh(_dot(lerp(3), w1_ref[...])).astype(BF16), w2_ref[...])
    w_log = -_softplus(-wl) - RW_DECAY_OFFSET
    a = _sigmoid(a0_ref[...] + _dot(_dot(lerp(4), a1_ref[...]).astype(BF16), a2_ref[...]))
    g = _dot(_sigmoid(_dot(lerp(5), g1_ref[...])).astype(BF16), g2_ref[...])

    kk = k * kk_ref[...]
    ones = _head_sum_matrix(1.0)
    parts = []
    for p in range(kk.shape[1] // V7X_LANES):
        kp = kk[:, p * V7X_LANES:(p + 1) * V7X_LANES]
        parts.append(kp * lax.rsqrt(jnp.maximum(_dot_exact_rhs(kp * kp, ones), 1e-24)))
    r_out[...] = r
    ld_out[...] = -jnp.exp(w_log)
    k_out[...] = k * (1.0 + (a - 1.0) * ka_ref[...])
    v_out[...] = v
    kkn_out[...] = jnp.concatenate(parts, axis=1)
    a_out[...] = a
    g_out[...] = g


def _rw_proj(h, g, mix, w_r, w_k, w_v, w0, w1, w2, a0, a1, a2, g1, g2, k_k, k_a, *, seq, tm):
    t, d = h.shape
    tm = min(tm, t)
    row = pl.BlockSpec((tm, d), lambda i: (i, 0))
    full = lambda x: pl.BlockSpec(x.shape, lambda i: (0,) * x.ndim)
    prev = pl.BlockSpec((8, d), lambda i: (jnp.maximum(i * (tm // 8) - 1, 0), 0))
    out = jax.ShapeDtypeStruct((t, d), F32)
    small = (g, mix, w_r, w_k, w_v, w0, w1, w2, a0, a1, a2, g1, g2, k_k, k_a)
    return pl.pallas_call(
        functools.partial(_rw_proj_kernel, tm=tm, seq=seq),
        grid=(t // tm,),
        in_specs=[row, prev] + [full(x) for x in small],
        out_specs=[row] * 7,
        out_shape=[out] * 7,
        name="rw_proj",
        compiler_params=_params("parallel"),
    )(h, h, *small)


RW_CHUNK = 64


def _rw_scan_kernel(r_ref, ld_ref, k_ref, v_ref, kk_ref, a_ref, g_ref, rk_ref, lng_ref, lnb_ref,
                    o_ref, st_sc):
    c_len = RW_CHUNK
    n_pairs = r_ref.shape[2] // V7X_LANES

    @pl.when(pl.program_id(1) == 0)
    def _():
        st_sc[...] = jnp.zeros_like(st_sc)

    n2 = 2 * c_len
    ri, ci = _iota((n2, n2), 0), _iota((n2, n2), 1)
    strict = ci < ri
    incl = ci <= ri
    eye = (ci == ri).astype(F32)
    tri = (_iota((c_len, c_len), 0) >= _iota((c_len, c_len), 1)).astype(BF16)
    lane = _iota((c_len, V7X_LANES), 1)
    in_head = [lane < RW_HEAD_DIM, lane >= RW_HEAD_DIM]
    head_mean = _head_sum_matrix(1.0 / RW_HEAD_DIM)
    head_sum = _head_sum_matrix(1.0)

    def stack(x):
        return jnp.concatenate([jnp.where(m, x, 0.0) for m in in_head], axis=0)

    pairs = range(n_pairs)
    chunks = range(r_ref.shape[1] // c_len)
    items = [(c, p) for c in chunks for p in pairs]
    rows = [slice(c * c_len, (c + 1) * c_len) for c in chunks]
    cols = [slice(p * V7X_LANES, (p + 1) * V7X_LANES) for p in pairs]
    ld_all = [ld_ref[0, rows[c], :] for c in chunks]
    tri3 = jnp.concatenate([tri, tri, tri], axis=1)
    b_all = [_dot(tri3, jnp.concatenate(_split3(ld_all[c]), axis=0)) for c in chunks]

    lhs, rhs, v_st, end_rows = {}, {}, {}, {}
    for c, p in items:
        at = lambda ref: ref[0, rows[c], cols[p]]
        r, k, kk = at(r_ref), at(k_ref), at(kk_ref)
        b = b_all[c][:, cols[p]]
        beta = kk * at(a_ref)
        grow, decay, to_end = jnp.exp(-b), jnp.exp(b), jnp.exp(b[c_len - 1:c_len] - b)
        lhs[c, p] = jnp.concatenate(
            [stack(-kk * jnp.exp(b - ld_all[c][:, cols[p]])), stack(r * decay)], axis=0).astype(BF16)
        rhs[c, p] = jnp.concatenate([stack(beta * grow), stack(k * grow)], axis=0).astype(BF16)
        v_st[c, p] = stack(at(v_ref)).astype(BF16)
        end_rows[c, p] = jnp.concatenate([stack(beta * to_end), stack(k * to_end)], axis=0).astype(BF16)

    aa = {i: _dot_nt(lhs[i], rhs[i]) for i in items}
    a_ab = {i: jnp.where(strict, aa[i][:n2, :n2], 0.0) for i in items}
    a_k = {i: jnp.concatenate([jnp.where(strict, aa[i][:n2, n2:], 0.0),
                               jnp.where(incl, aa[i][n2:, n2:], 0.0)], axis=0).astype(BF16) for i in items}
    a_rb = {i: jnp.where(incl, aa[i][n2:, :n2], 0.0).astype(BF16) for i in items}
    by_v = {i: _dot(a_k[i], v_st[i]) for i in items}

    t_inv = {i: eye + a_ab[i] for i in items}
    xb = {i: a_ab[i].astype(BF16) for i in items}
    xb = {i: _dot(xb[i], xb[i]).astype(BF16) for i in items}
    for _ in range(c_len.bit_length() - 3):
        both = {i: _dot(jnp.concatenate([xb[i], t_inv[i].astype(BF16)], axis=0), xb[i]) for i in items}
        t_inv = {i: t_inv[i] + both[i][n2:] for i in items}
        xb = {i: both[i][:n2].astype(BF16) for i in items}
    t_inv = {i: t_inv[i] + _dot(t_inv[i].astype(BF16), xb[i]) for i in items}

    st_t = [st_sc[p] for p in pairs]
    y = {}
    for c in chunks:
        by_state = [_dot_nt(lhs[c, p], st_t[p].astype(BF16)) for p in pairs]
        sa = [_dot(t_inv[c, p].astype(BF16), (by_state[p][:n2] + by_v[c, p][:n2]).astype(BF16))
              for p in pairs]
        y_st = [by_state[p][n2:] + by_v[c, p][n2:] + _dot(a_rb[c, p], sa[p].astype(BF16)) for p in pairs]
        st_t = [st_t[p] * jnp.exp(b_all[c][c_len - 1:c_len, cols[p]]) + _dot_tn(
            jnp.concatenate([sa[p].astype(BF16), v_st[c, p]], axis=0), end_rows[c, p]) for p in pairs]
        for p in pairs:
            y[c, p] = y_st[p][:c_len] + y_st[p][c_len:]
    for p in pairs:
        st_sc[p] = st_t[p]

    sums = {(c, p): _dot(jnp.concatenate(
        [y[c, p], r_ref[0, rows[c], cols[p]] * k_ref[0, rows[c], cols[p]] * rk_ref[:, cols[p]]],
        axis=0).astype(BF16), head_sum) for c, p in items}
    dlt = {i: y[i] - sums[i][:c_len] * (1.0 / RW_HEAD_DIM) for i in items}
    var = {i: _dot((dlt[i] * dlt[i]).astype(BF16), head_mean) for i in items}
    for c, p in items:
        yn = dlt[c, p] * lax.rsqrt(var[c, p] + RW_GN_EPS) * lng_ref[:, cols[p]] + lnb_ref[:, cols[p]]
        o_ref[0, rows[c], cols[p]] = ((yn + sums[c, p][c_len:] * v_ref[0, rows[c], cols[p]])
                                      * g_ref[0, rows[c], cols[p]]).astype(o_ref.dtype)


def _rw_scan(r, ld, k, v, kk, a, g, r_k, ln_g, ln_b, *, chunks_per_step):
    b, s, d = r.shape
    assert 2 * RW_HEAD_DIM == V7X_LANES
    step_rows = min(chunks_per_step * RW_CHUNK, s)
    blk = pl.BlockSpec((1, step_rows, d), lambda bi, c: (bi, c, 0))
    vec = pl.BlockSpec((1, d), lambda bi, c: (0, 0))
    return pl.pallas_call(
        _rw_scan_kernel,
        grid=(b, s // step_rows),
        in_specs=[blk] * 7 + [vec] * 3,
        out_specs=blk,
        out_shape=jax.ShapeDtypeStruct((b, s, d), BF16),
        scratch_shapes=[pltpu.VMEM((d // V7X_LANES, V7X_LANES, V7X_LANES), F32)],
        name="rw_scan",
        compiler_params=_params("parallel", "arbitrary"),
    )(r, ld, k, v, kk, a, g, r_k, ln_g, ln_b)


SC_HALO = 16


def _sc_kernel(h_ref, hp_ref, g_ref, w_ref, cw_ref, cb_ref, o_ref, *, tm, seq):
    i = pl.program_id(0)
    d = h_ref.shape[1]
    gn = g_ref[...]
    xn = _rms(h_ref[...], gn)
    xe = jnp.concatenate([_rms(hp_ref[...], gn), xn], axis=0).astype(BF16)
    gate = _dot(xn.astype(BF16), w_ref[:, 0:d])
    ch = _dot(xe, w_ref[:, d:3 * d])
    u = ch[:, :d] * ch[:, d:]
    halo_is_padding = (i * tm) % seq == 0
    u = jnp.where(jnp.logical_and(_iota(u.shape, 0) < SC_HALO, halo_is_padding), 0.0, u)
    y = (cw_ref[2:3] * u[SC_HALO:]
         + cw_ref[1:2] * pltpu.roll(u, 1, 0)[SC_HALO:]
         + cw_ref[0:1] * pltpu.roll(u, 2, 0)[SC_HALO:]
         + cb_ref[...])
    o_ref[...] = (gate * y).astype(o_ref.dtype)


def _short_conv(h, g, w_in, conv_w, conv_b, *, seq, tm):
    t, d = h.shape
    tm = min(tm, t)
    row = pl.BlockSpec((tm, d), lambda i: (i, 0))
    full = lambda x: pl.BlockSpec(x.shape, lambda i: (0,) * x.ndim)
    prev = pl.BlockSpec((SC_HALO, d), lambda i: (jnp.maximum(i * (tm // SC_HALO) - 1, 0), 0))
    return pl.pallas_call(
        functools.partial(_sc_kernel, tm=tm, seq=seq),
        grid=(t // tm,),
        in_specs=[row, prev, full(g), full(w_in), full(conv_w), full(conv_b)],
        out_specs=row,
        out_shape=jax.ShapeDtypeStruct((t, d), BF16),
        name="short_conv",
        compiler_params=_params("parallel"),
    )(h, h, g, w_in, conv_w, conv_b)


def kernel(x, norm_mix_g, norm_ffn_g, ffn_w_up, ffn_w_down, final_norm_g, sb_w_qkv, sb_w_out, hg_w_in, hg_lower_bounds, hg_norm_g, hg_w_out, rw_mix, rw_w_in, rw_w0, rw_w1, rw_w2, rw_a0, rw_a1, rw_a2, rw_g1, rw_g2, rw_k_k, rw_k_a, rw_r_k, rw_ln_g, rw_ln_b, rw_w_out, sc_w_in, sc_conv_w, sc_conv_b, sc_w_out):
    bsz, seq, d = x.shape
    depth = norm_mix_g.shape[0]
    n_mixers = 4
    t = bsz * seq
    bf = lambda w: w.astype(BF16)
    row = lambda v: v.reshape(1, -1)
    seq3 = lambda y: y.reshape(bsz, seq, d)

    h = x.reshape(t, d)
    for i in range(depth):
        m, j = i % n_mixers, i // n_mixers
        g_mix = row(norm_mix_g[i])
        if m == 0:
            qkv = _norm_proj(h, g_mix, bf(sb_w_qkv[j]), out_dtype=BF16, **TILES["qkv_proj"])
            y = _sb_attention(qkv.reshape(bsz, seq, 3 * d), n_heads=d // SB_HEAD_DIM, **TILES["sb_attention"])
            w_o = sb_w_out[j]
        elif m == 1:
            q, lf, kf, v, gate = _hg_proj(h, g_mix, hg_lower_bounds, bf(hg_w_in[j]), layer_idx=i,
                                          **TILES["hg_proj"])
            y = _hg_scan(seq3(q), seq3(lf), seq3(kf), seq3(v), seq3(gate), row(hg_norm_g[j]),
                         **TILES["hg_scan"])
            w_o = hg_w_out[j]
        elif m == 2:
            parts = _rw_proj(h, g_mix, rw_mix[j], bf(rw_w_in[j, 0]), bf(rw_w_in[j, 1]), bf(rw_w_in[j, 2]),
                             row(rw_w0[j]), bf(rw_w1[j]), bf(rw_w2[j]), row(rw_a0[j]), bf(rw_a1[j]),
                             bf(rw_a2[j]), bf(rw_g1[j]), bf(rw_g2[j]), row(rw_k_k[j]), row(rw_k_a[j]),
                             seq=seq, **TILES["rw_proj"])
            y = _rw_scan(*[seq3(p) for p in parts], row(rw_r_k[j]), row(rw_ln_g[j]), row(rw_ln_b[j]),
                         **TILES["rw_scan"])
            w_o = rw_w_out[j]
        else:
            y = _short_conv(h, g_mix, bf(sc_w_in[j]), sc_conv_w[j], row(sc_conv_b[j]), seq=seq,
                            **TILES["short_conv"])
            w_o = sc_w_out[j]
        h = _mixer_out_mlp(h, y.reshape(t, d), bf(w_o), row(norm_ffn_g[i]), bf(ffn_w_up[i]),
                           bf(ffn_w_down[i]), row(final_norm_g), final_norm=(i == depth - 1),
                           **TILES["mixer_out_mlp"])
    return h.reshape(bsz, seq, d)
```
